```python
import math
import jax, jax.numpy as jnp
from jax import lax
import numpy as np

D_MODEL = 1024
BATCH = 4
SEQ = 8192
DEPTH = 2

CTX_LEN = 256
GRID_W = 64
CHUNK = 64
N_MIXERS = 2
EPS = 1e-6

DN_DK = 128
DN_HEADS = D_MODEL // DN_DK
DN_DV = 2 * DN_DK
DN_QK = DN_HEADS * DN_DK
DN_V = DN_HEADS * DN_DV
DN_QKV = 2 * DN_QK + DN_V
DN_IN = DN_QKV + DN_V + 4 * DN_HEADS
CONV_K = 3

GLA_HEADS = 4
GLA_DK = D_MODEL // (2 * GLA_HEADS)
GLA_DV = 2 * D_MODEL // GLA_HEADS
GLA_K = GLA_HEADS * GLA_DK
GLA_V = GLA_HEADS * GLA_DV
GLA_RANK = 16
GLA_TAU = 16.0
GLA_IN = 2 * GLA_K + 2 * GLA_V + 2 * GLA_RANK

N_DN = (DEPTH + 1) // 2
N_GLA = DEPTH // 2

kernel_name = 'hybrid_deltanet_gla_prefix_dit'


def rms_norm(x, g):
    xf = x.astype(jnp.float32)
    y = xf * lax.rsqrt(jnp.mean(jnp.square(xf), axis=-1, keepdims=True) + EPS)
    return y * g.astype(jnp.float32)


def l2_norm(x):
    xf = x.astype(jnp.float32)
    return xf * lax.rsqrt(jnp.sum(jnp.square(xf), axis=-1, keepdims=True) + EPS)


def ada_modulation(cond, w_mod, b_mod):
    m = jax.nn.silu(cond.astype(jnp.float32)) @ w_mod + b_mod
    shift, scale, gate = jnp.split(m, 3, axis=-1)
    return shift[..., None, :], scale[..., None, :], gate[..., None, :]


def to_chunks(u):
    b, t = u.shape[:2]
    u = u.reshape((b, t // CHUNK, CHUNK) + u.shape[2:])
    return jnp.moveaxis(u, 3, 1)


def from_chunks(o):
    o = jnp.moveaxis(o, 1, 3)
    b, n, c = o.shape[:3]
    return o.reshape((b, n * c) + o.shape[3:])


def flip_seq(u, d):
    return jnp.flip(u, axis=1) if d == 1 else u


def short_conv_grid(u, w):
    b, t, ch = u.shape
    rows = t // GRID_W
    img = u.reshape(b, rows, GRID_W, ch)
    out = lax.conv_general_dilated(img, w[:, :, None, :].astype(u.dtype), (1, 1), 'SAME',
                                   dimension_numbers=('NHWC', 'HWIO', 'NHWC'), feature_group_count=ch)
    return out.reshape(b, t, ch)


def short_conv_seq(u, w):
    ch = u.shape[-1]
    k1 = w[CONV_K // 2][:, None, :].astype(u.dtype)
    return lax.conv_general_dilated(u, k1, (1,), 'SAME',
                                    dimension_numbers=('NWC', 'WIO', 'NWC'), feature_group_count=ch)


def delta_chunk_scan(q, k, v, g, beta, s0):
    c = q.shape[-2]
    dk = q.shape[-1]
    gc = jnp.cumsum(g, axis=-1)
    idx = jnp.arange(c)
    causal = idx[:, None] >= idx[None, :]
    strict = idx[:, None] > idx[None, :]
    gam = jnp.exp(jnp.where(causal, gc[..., :, None] - gc[..., None, :], -jnp.inf))
    kk = jnp.einsum('bhnck,bhnjk->bhncj', k, k)
    m = jnp.eye(c, dtype=jnp.float32) + jnp.where(strict, beta[..., :, None] * kk * gam, 0.0)
    rhs = jnp.concatenate([k * (beta * jnp.exp(gc))[..., None], v * beta[..., None]], axis=-1)
    sol = lax.linalg.triangular_solve(m, rhs, left_side=True, lower=True)
    w_c, u_c = sol[..., :dk], sol[..., dk:]
    a_qk = jnp.einsum('bhnck,bhnjk->bhncj', q, k) * gam
    q_dec = q * jnp.exp(gc)[..., None]
    k_dec = k * jnp.exp(gc[..., -1:] - gc)[..., None]
    c_dec = jnp.exp(gc[..., -1])
    xs = tuple(jnp.moveaxis(t, 2, 0) for t in (w_c, u_c, a_qk, q_dec, k_dec, c_dec))

    def step(s, inp):
        w_n, u_n, a_n, qd_n, kd_n, cd_n = inp
        v_new = u_n - jnp.einsum('bhck,bhkv->bhcv', w_n, s)
        o = jnp.einsum('bhck,bhkv->bhcv', qd_n, s) + jnp.einsum('bhcj,bhjv->bhcv', a_n, v_new)
        s = s * cd_n[..., None, None] + jnp.einsum('bhck,bhcv->bhkv', kd_n, v_new)
        return s, o

    s_fin, o = lax.scan(step, s0, xs)
    return jnp.moveaxis(o, 0, 2), s_fin


def gla_chunk_scan(q, k, v, lg, s0):
    c = q.shape[-2]
    bc = jnp.cumsum(lg, axis=-2)
    ref = bc[..., c // 2:c // 2 + 1, :]
    idx = jnp.arange(c)
    causal = idx[:, None] >= idx[None, :]
    a = jnp.einsum('bhnck,bhnjk->bhncj', q * jnp.exp(bc - ref), k * jnp.exp(ref - bc))
    a = jnp.where(causal, a, 0.0)
    o_intra = jnp.einsum('bhncj,bhnjv->bhncv', a, v)
    q_dec = q * jnp.exp(bc)
    k_dec = k * jnp.exp(bc[..., -1:, :] - bc)
    c_dec = jnp.exp(bc[..., -1, :])
    xs = tuple(jnp.moveaxis(t, 2, 0) for t in (q_dec, k_dec, v, c_dec))

    def step(s, inp):
        qd_n, kd_n, v_n, cd_n = inp
        o = jnp.einsum('bhck,bhkv->bhcv', qd_n, s)
        s = s * cd_n[..., :, None] + jnp.einsum('bhck,bhcv->bhkv', kd_n, v_n)
        return s, o

    s_fin, o_inter = lax.scan(step, s0, xs)
    return jnp.moveaxis(o_inter, 0, 2) + o_intra, s_fin


def bidir_prefix_scan(core, ctx_dirs, lat_dirs, s0, need_ctx):
    o_ctx, o_lat = [], []
    for d in range(2):
        fc = tuple(to_chunks(flip_seq(u, d)) for u in ctx_dirs[d])
        fl = tuple(to_chunks(flip_seq(u, d)) for u in lat_dirs[d])
        oc, s_ctx = core(*fc, s0)
        ol, _ = core(*fl, s_ctx)
        o_lat.append(flip_seq(from_chunks(ol), d))
        if need_ctx:
            o_ctx.append(flip_seq(from_chunks(oc), d))
    return (o_ctx[0] + o_ctx[1]) if need_ctx else None, o_lat[0] + o_lat[1]


def gated_out(o, z, norm_g, w_out):
    b, t = o.shape[:2]
    y = rms_norm(o, norm_g).reshape(b, t, -1) * jax.nn.silu(z.astype(jnp.float32))
    return y @ w_out


def deltanet_features(h, w_in, conv_w, a_log, dt_bias, on_grid):
    b, t, _ = h.shape
    p = h @ w_in
    qkv, z, a, bt = jnp.split(p, [DN_QKV, DN_QKV + DN_V, DN_QKV + DN_V + 2 * DN_HEADS], axis=-1)
    qkv = jax.nn.silu(short_conv_grid(qkv, conv_w) if on_grid else short_conv_seq(qkv, conv_w))
    q, k, v = jnp.split(qkv, [DN_QK, 2 * DN_QK], axis=-1)
    q = l2_norm(q.reshape(b, t, DN_HEADS, DN_DK)) * (DN_DK ** -0.5)
    k = l2_norm(k.reshape(b, t, DN_HEADS, DN_DK))
    v = v.reshape(b, t, DN_HEADS, DN_DV).astype(jnp.float32)
    a = a.reshape(b, t, 2, DN_HEADS).astype(jnp.float32)
    bt = bt.reshape(b, t, 2, DN_HEADS).astype(jnp.float32)
    g = -jnp.exp(a_log.astype(jnp.float32)) * jax.nn.softplus(a + dt_bias.astype(jnp.float32))
    beta = jax.nn.sigmoid(bt)
    dirs = [(q, k, v, g[:, :, d], beta[:, :, d]) for d in range(2)]
    return dirs, z


def deltanet_mixer(h_ctx, h_lat, w_in, conv_w, a_log, dt_bias, norm_g, w_out, need_ctx):
    ctx_dirs, z_ctx = deltanet_features(h_ctx, w_in, conv_w, a_log, dt_bias, False)
    lat_dirs, z_lat = deltanet_features(h_lat, w_in, conv_w, a_log, dt_bias, True)
    s0 = jnp.zeros((h_lat.shape[0], DN_HEADS, DN_DK, DN_DV), jnp.float32)
    o_ctx, o_lat = bidir_prefix_scan(delta_chunk_scan, ctx_dirs, lat_dirs, s0, need_ctx)
    y_lat = gated_out(o_lat, z_lat, norm_g, w_out)
    y_ctx = gated_out(o_ctx, z_ctx, norm_g, w_out) if need_ctx else None
    return y_ctx, y_lat


def gla_features(h, w_in, w_g2, b_g):
    b, t, _ = h.shape
    p = h @ w_in
    q, k, v, r, gl = jnp.split(p, [GLA_K, 2 * GLA_K, 2 * GLA_K + GLA_V, 2 * GLA_K + 2 * GLA_V], axis=-1)
    q = q.reshape(b, t, GLA_HEADS, GLA_DK).astype(jnp.float32) * (GLA_DK ** -0.5)
    k = k.reshape(b, t, GLA_HEADS, GLA_DK).astype(jnp.float32)
    v = v.reshape(b, t, GLA_HEADS, GLA_DV).astype(jnp.float32)
    gl = gl.reshape(b, t, 2, GLA_RANK)
    zg = jnp.einsum('btdr,drk->btdk', gl, w_g2) + b_g
    lg = (jax.nn.log_sigmoid(zg.astype(jnp.float32)) / GLA_TAU).reshape(b, t, 2, GLA_HEADS, GLA_DK)
    dirs = [(q, k, v, lg[:, :, d]) for d in range(2)]
    return dirs, r


def gla_mixer(h_ctx, h_lat, w_in, w_g2, b_g, norm_g, w_out, need_ctx):
    ctx_dirs, r_ctx = gla_features(h_ctx, w_in, w_g2, b_g)
    lat_dirs, r_lat = gla_features(h_lat, w_in, w_g2, b_g)
    s0 = jnp.zeros((h_lat.shape[0], GLA_HEADS, GLA_DK, GLA_DV), jnp.float32)
    o_ctx, o_lat = bidir_prefix_scan(gla_chunk_scan, ctx_dirs, lat_dirs, s0, need_ctx)
    y_lat = gated_out(o_lat, r_lat, norm_g, w_out)
    y_ctx = gated_out(o_ctx, r_ctx, norm_g, w_out) if need_ctx else None
    return y_ctx, y_lat


def setup_inputs(seed: int = 0) -> dict:
    key = jax.random.key(seed)
    ks = jax.random.split(key, 20)
    f32 = jnp.float32

    def nrm(k, shape, s):
        return jax.random.normal(k, shape, f32) * s

    x = nrm(ks[0], (BATCH, SEQ, D_MODEL), 1.0)
    c = nrm(ks[1], (BATCH, D_MODEL), 1.0)
    ctx = nrm(ks[2], (BATCH, CTX_LEN, D_MODEL), 1.0)
    c_ctx = nrm(ks[3], (D_MODEL,), 1.0)
    mod_w = nrm(ks[4], (DEPTH, D_MODEL, 3 * D_MODEL), 0.5 * D_MODEL ** -0.5)
    mod_b = nrm(ks[5], (DEPTH, 3 * D_MODEL), 0.02)
    norm_g = 1.0 + nrm(ks[6], (DEPTH, D_MODEL), 0.02)
    dn_w_in = nrm(ks[7], (N_DN, D_MODEL, DN_IN), D_MODEL ** -0.5)
    dn_conv_w = nrm(ks[8], (N_DN, CONV_K, CONV_K, DN_QKV), 1.0 / CONV_K)
    dn_a_log = jnp.log(jax.random.uniform(ks[9], (N_DN, 2, DN_HEADS), f32, 1.0, 16.0))
    dt = jnp.exp(jax.random.uniform(ks[10], (N_DN, 2, DN_HEADS), f32, math.log(1e-3), math.log(1e-1)))
    dn_dt_bias = dt + jnp.log(-jnp.expm1(-dt))
    dn_norm_g = 1.0 + nrm(ks[11], (N_DN, DN_DV), 0.02)
    dn_w_out = nrm(ks[12], (N_DN, DN_V, D_MODEL), DN_V ** -0.5)
    gla_w_in = nrm(ks[13], (N_GLA, D_MODEL, GLA_IN), D_MODEL ** -0.5)
    gla_w_g2 = nrm(ks[14], (N_GLA, 2, GLA_RANK, GLA_K), GLA_RANK ** -0.5)
    gla_b_g = nrm(ks[15], (N_GLA, 2, GLA_K), 0.1)
    gla_norm_g = 1.0 + nrm(ks[16], (N_GLA, GLA_DV), 0.02)
    gla_w_out = nrm(ks[17], (N_GLA, GLA_V, D_MODEL), GLA_V ** -0.5)
    final_g = 1.0 + nrm(ks[18], (D_MODEL,), 0.02)
    return {'x': x, 'c': c, 'ctx': ctx, 'c_ctx': c_ctx, 'mod_w': mod_w, 'mod_b': mod_b, 'norm_g': norm_g,
            'dn_w_in': dn_w_in, 'dn_conv_w': dn_conv_w, 'dn_a_log': dn_a_log, 'dn_dt_bias': dn_dt_bias,
            'dn_norm_g': dn_norm_g, 'dn_w_out': dn_w_out, 'gla_w_in': gla_w_in, 'gla_w_g2': gla_w_g2,
            'gla_b_g': gla_b_g, 'gla_norm_g': gla_norm_g, 'gla_w_out': gla_w_out, 'final_g': final_g}


def reference(x, c, ctx, c_ctx, mod_w, mod_b, norm_g, dn_w_in, dn_conv_w, dn_a_log, dn_dt_bias,
              dn_norm_g, dn_w_out, gla_w_in, gla_w_g2, gla_b_g, gla_norm_g, gla_w_out, final_g):
    for i in range(DEPTH):
        need_ctx = i < DEPTH - 1
        sh_l, sc_l, gt_l = ada_modulation(c, mod_w[i], mod_b[i])
        sh_c, sc_c, gt_c = ada_modulation(c_ctx, mod_w[i], mod_b[i])
        h_lat = rms_norm(x, norm_g[i]) * (1.0 + sc_l) + sh_l
        h_ctx = rms_norm(ctx, norm_g[i]) * (1.0 + sc_c) + sh_c
        j = i // N_MIXERS
        if i % N_MIXERS == 0:
            y_ctx, y_lat = deltanet_mixer(h_ctx, h_lat, dn_w_in[j], dn_conv_w[j], dn_a_log[j], dn_dt_bias[j],
                                          dn_norm_g[j], dn_w_out[j], need_ctx)
        else:
            y_ctx, y_lat = gla_mixer(h_ctx, h_lat, gla_w_in[j], gla_w_g2[j], gla_b_g[j],
                                     gla_norm_g[j], gla_w_out[j], need_ctx)
        x = x + gt_l * y_lat
        if need_ctx:
            ctx = ctx + gt_c * y_ctx
    return rms_norm(x, final_g)
```

```python
import functools

import numpy as np
import jax
import jax.numpy as jnp
from jax import lax
from jax.experimental import pallas as pl
from jax.experimental.pallas import tpu as pltpu

F32, BF16 = jnp.float32, jnp.bfloat16
HIGHEST = lax.Precision.HIGHEST
EPS = 1e-6

LANES = 128
GRID_W = 64
CHUNK = 128
SUB = 64
VMEM_LIMIT_BYTES = 52 * 1024 * 1024

DN_DK, DN_HEADS, DN_DV = 128, 8, 256
DN_QK = DN_HEADS * DN_DK
DN_V = DN_HEADS * DN_DV
DN_QKV = 2 * DN_QK + DN_V
DN_MAIN = DN_QKV + DN_V
GLA_HEADS, GLA_DK, GLA_DV, GLA_RANK, GLA_TAU = 4, 128, 512, 16, 16.0
GLA_K = GLA_HEADS * GLA_DK
GLA_V = GLA_HEADS * GLA_DV
GLA_MAIN = 2 * GLA_K + 2 * GLA_V


def _sigmoid(x):
    return 1.0 / (1.0 + jnp.exp(-x))


def _silu(x):
    return x * _sigmoid(x)


def _softplus(x):
    return jnp.maximum(x, 0.0) + jnp.log(1.0 + jnp.exp(-jnp.abs(x)))


def _mm(a, b):
    return jnp.dot(a.astype(BF16), b.astype(BF16), preferred_element_type=F32)


def _mm_nt(a, b):
    return lax.dot_general(a.astype(BF16), b.astype(BF16), (((1,), (1,)), ((), ())),
                           preferred_element_type=F32)


def _mm_f32(a, b):
    return jnp.dot(a, b, precision=HIGHEST, preferred_element_type=F32)


def _params(semantics):
    return pltpu.CompilerParams(dimension_semantics=semantics, vmem_limit_bytes=VMEM_LIMIT_BYTES)


def _mod_kernel(c_ref, w_ref, b_ref, o_ref):
    o_ref[0] = _mm_f32(_silu(c_ref[...]), w_ref[0]) + b_ref[0]


def _modulation(cond, mod_w, mod_b):
    depth, d, d3 = mod_w.shape
    tn = d
    return pl.pallas_call(
        _mod_kernel,
        grid=(depth, d3 // tn),
        in_specs=[pl.BlockSpec((8, d), lambda l, j: (0, 0)),
                  pl.BlockSpec((1, d, tn), lambda l, j: (l, 0, j)),
                  pl.BlockSpec((1, 1, tn), lambda l, j: (l, 0, j))],
        out_specs=pl.BlockSpec((1, 8, tn), lambda l, j: (l, 0, j)),
        out_shape=jax.ShapeDtypeStruct((depth, 8, d3), F32),
        compiler_params=_params(("parallel", "parallel")),
        name="modulation",
    )(cond, mod_w, mod_b.reshape(depth, 1, d3))


def _norm_mod(x, g, sc, sh):
    ms = jnp.mean(x * x, axis=-1, keepdims=True)
    return x * lax.rsqrt(ms + EPS) * g * (1.0 + sc) + sh


def _dn_inproj_kernel(x_ref, sh_ref, sc_ref, g_ref, w_ref, ws_ref, ap_ref, o_ref, gates_ref, h_scr):
    @pl.when(pl.program_id(2) == 0)
    def _():
        hb = _norm_mod(x_ref[0], g_ref[...], sc_ref[0], sh_ref[0]).astype(BF16)
        h_scr[...] = hb
        raw = jnp.dot(hb, ws_ref[...], preferred_element_type=F32)
        decay = -jnp.exp(ap_ref[0:1, :]) * _softplus(raw + ap_ref[1:2, :])
        lane = lax.broadcasted_iota(jnp.int32, raw.shape, 1)
        gates_ref[0] = jnp.where(lane < 2 * DN_HEADS, decay, _sigmoid(raw))

    o_ref[0] = jnp.dot(h_scr[...], w_ref[...], preferred_element_type=F32).astype(o_ref.dtype)


def _gla_inproj_kernel(x_ref, sh_ref, sc_ref, g_ref, w_ref, ws_ref, wg_ref, bg_ref, o_ref, lg_ref, h_scr):
    @pl.when(pl.program_id(2) == 0)
    def _():
        hb = _norm_mod(x_ref[0], g_ref[...], sc_ref[0], sh_ref[0]).astype(BF16)
        h_scr[...] = hb
        gl = jnp.dot(hb, ws_ref[...], preferred_element_type=F32)
        zg = jnp.dot(gl.astype(BF16), wg_ref[...], preferred_element_type=F32) + bg_ref[...]
        lg_ref[0] = -_softplus(-zg) * (1.0 / GLA_TAU)

    o_ref[0] = jnp.dot(h_scr[...], w_ref[...], preferred_element_type=F32).astype(o_ref.dtype)


def _inproj(kernel_fn, name, x, sh, sc, g, w_main, w_small, extra, aux_width):
    bsz, t, d = x.shape
    n = w_main.shape[1]
    tm = min(t, 1024)
    tn = 512
    const = lambda b, i, j: (0, 0)
    extra_specs = [pl.BlockSpec(e.shape, const) for e in extra]
    return pl.pallas_call(
        kernel_fn,
        grid=(bsz, t // tm, n // tn),
        in_specs=[pl.BlockSpec((1, tm, d), lambda b, i, j: (b, i, 0)),
                  pl.BlockSpec((1, 1, d), lambda b, i, j: (b, 0, 0)),
                  pl.BlockSpec((1, 1, d), lambda b, i, j: (b, 0, 0)),
                  pl.BlockSpec((1, d), const),
                  pl.BlockSpec((d, tn), lambda b, i, j: (0, j)),
                  pl.BlockSpec((d, LANES), const)] + extra_specs,
        out_specs=[pl.BlockSpec((1, tm, tn), lambda b, i, j: (b, i, j)),
                   pl.BlockSpec((1, tm, aux_width), lambda b, i, j: (b, i, 0))],
        out_shape=[jax.ShapeDtypeStruct((bsz, t, n), BF16),
                   jax.ShapeDtypeStruct((bsz, t, aux_width), F32)],
        scratch_shapes=[pltpu.VMEM((tm, d), BF16)],
        compiler_params=_params(("parallel", "parallel", "arbitrary")),
        name=name,
    )(x, sh, sc, g, w_main, w_small, *extra)


def _conv_kernel(prev_ref, main_ref, next_ref, w_ref, o_ref, *, on_grid, n_q_tiles, n_qk_tiles):
    i = pl.program_id(1)
    j = pl.program_id(2)
    x = main_ref[0].astype(F32)
    tt, tc = x.shape
    w = w_ref[...]
    tok = lax.broadcasted_iota(jnp.int32, (tt, tc), 0)
    if on_grid:
        prev = jnp.where(i > 0, prev_ref[0].astype(F32), 0.0)
        nxt = jnp.where(i < pl.num_programs(1) - 1, next_ref[0].astype(F32), 0.0)
        ext = jnp.concatenate([prev, x, nxt], axis=0)
        rows = [ext[0:tt], x, ext[2 * GRID_W:2 * GRID_W + tt]]
        accs = [w[dc:dc + 1] * rows[0] + w[3 + dc:4 + dc] * rows[1] + w[6 + dc:7 + dc] * rows[2]
                for dc in range(3)]
        pos = tok & (GRID_W - 1)
        first, last = pos == 0, pos == GRID_W - 1
    else:
        accs = [w[3 + dc:4 + dc] * x for dc in range(3)]
        first, last = tok == 0, tok == tt - 1
    y = (accs[1] + jnp.where(first, 0.0, pltpu.roll(accs[0], 1, 0))
         + jnp.where(last, 0.0, pltpu.roll(accs[2], tt - 1, 0)))
    y = _silu(y)

    @pl.when(j < n_qk_tiles)
    def _():
        scale = jnp.where(j < n_q_tiles, DN_DK ** -0.5, 1.0)
        for s in range(tc // DN_DK):
            seg = y[:, s * DN_DK:(s + 1) * DN_DK]
            ss = jnp.sum(seg * seg, axis=-1, keepdims=True)
            o_ref[0, :, s * DN_DK:(s + 1) * DN_DK] = (seg * (lax.rsqrt(ss + EPS) * scale)).astype(o_ref.dtype)

    @pl.when(j >= n_qk_tiles)
    def _():
        o_ref[0] = y.astype(o_ref.dtype)


def _conv(p, conv_w9, on_grid):
    bsz, t, _ = p.shape
    tc = 512
    tt = min(t, 512) if on_grid else t
    rows_per_tile = tt // GRID_W
    n_rows = t // GRID_W
    kern = functools.partial(_conv_kernel, on_grid=on_grid, n_q_tiles=DN_QK // tc, n_qk_tiles=2 * DN_QK // tc)
    return pl.pallas_call(
        kern,
        grid=(bsz, t // tt, DN_QKV // tc),
        in_specs=[pl.BlockSpec((1, GRID_W, tc), lambda b, i, j: (b, jnp.maximum(i * rows_per_tile - 1, 0), j)),
                  pl.BlockSpec((1, tt, tc), lambda b, i, j: (b, i, j)),
                  pl.BlockSpec((1, GRID_W, tc),
                               lambda b, i, j: (b, jnp.minimum((i + 1) * rows_per_tile, n_rows - 1), j)),
                  pl.BlockSpec((9, tc), lambda b, i, j: (0, j))],
        out_specs=pl.BlockSpec((1, tt, tc), lambda b, i, j: (b, i, j)),
        out_shape=jax.ShapeDtypeStruct((bsz, t, DN_QKV), BF16),
        compiler_params=_params(("parallel", "parallel", "parallel")),
        name="conv_grid" if on_grid else "conv_seq",
    )(p, p, p, conv_w9)


_BASE_BLOCK = 8
_M_EYE, _M_DIAG, _M_MERGE = 4, 5, 6


def _delta_mask_array():
    r = np.arange(CHUNK)[:, None]
    c = np.arange(CHUNK)[None, :]
    mats = [r >= c, r > c, r <= c, r < c, r == c, (r // _BASE_BLOCK) == (c // _BASE_BLOCK)]
    b = _BASE_BLOCK
    while b < CHUNK:
        mats.append(((r // (2 * b)) == (c // (2 * b))) & ((r // b) != (c // b)))
        b *= 2
    return jnp.asarray(np.stack(mats).astype(np.float32))


def _gla_mask_array():
    r = np.arange(CHUNK)[:, None]
    c = np.arange(CHUNK)[None, :]
    same = (r // SUB) == (c // SUB)
    mats = [(r >= c) & same, (r // SUB > c // SUB), (r <= c) & same, (r // SUB < c // SUB)]
    return jnp.asarray(np.stack(mats).astype(np.float32))


def _unit_tri_inverse(lm, m_ref):
    lb = lm * m_ref[_M_DIAG]
    l2 = _mm(lb, lb)
    l4 = _mm(l2, l2)
    x = m_ref[_M_EYE] - lb
    x = x + _mm(x, l2)
    x = x + _mm(x, l4)
    for n in range(_M_MERGE, m_ref.shape[0]):
        x = x - _mm(_mm(x, lm * m_ref[n]), x)
    return x


def _delta_prep(q, k, v, gates, lane_g, lane_b, rev, m_ref):
    lane = lax.broadcasted_iota(jnp.int32, gates.shape, 1)
    g = jnp.sum(jnp.where(lane == lane_g, gates, 0.0), axis=1, keepdims=True)
    beta = jnp.broadcast_to(jnp.sum(jnp.where(lane == lane_b, gates, 0.0), axis=1, keepdims=True),
                            (CHUNK, LANES))
    tri = m_ref[2 * rev]
    gc = _mm_f32(tri, jnp.broadcast_to(g, (CHUNK, LANES)))
    end = 0 if rev else CHUNK - 1
    gc_end = gc[end:end + 1, :]
    e_pos = jnp.exp(gc)
    e_end = jnp.exp(gc_end - gc)
    cdec = jnp.exp(gc_end)
    decay = jnp.exp(jnp.minimum(gc - gc.T, 0.0))
    kq = _mm_nt(jnp.concatenate([k, q], axis=0), k)
    lm = kq[:CHUNK] * decay * m_ref[2 * rev + 1] * beta
    a = (kq[CHUNK:] * decay * tri).astype(BF16)
    x = _unit_tri_inverse(lm, m_ref)
    kf = k.astype(F32)
    rhs = jnp.concatenate([(kf * (beta * e_pos)).astype(BF16),
                           (v.astype(F32) * jnp.concatenate([beta] * (DN_DV // LANES), axis=1)).astype(BF16)],
                          axis=1)
    wu = jnp.dot(x.astype(BF16), rhs, preferred_element_type=F32)
    w = wu[:, :DN_DK].astype(BF16)
    u = wu[:, DN_DK:].astype(BF16)
    kdt = (kf * e_end).T.astype(BF16)
    aw = jnp.dot(jnp.concatenate([kdt, a], axis=0), w, preferred_element_type=F32)
    top = jnp.concatenate([(-aw[:DN_DK]).astype(BF16), kdt], axis=1)
    bot = jnp.concatenate([(q.astype(F32) * e_pos - aw[DN_DK:]).astype(BF16), a], axis=1)
    return jnp.concatenate([top, bot], axis=0), u, cdec


def _delta_scan_kernel(qf_ref, kf_ref, vf_ref, gf_ref, qb_ref, kb_ref, vb_ref, gb_ref, s0_ref, m_ref,
                       of_ref, ob_ref, sfin_ref, s_scr, *, nc):
    h = pl.program_id(1)
    step = pl.program_id(2)

    @pl.when(step == 0)
    def _():
        s_scr[...] = s0_ref[0, 0]

    def one(q_ref, k_ref, v_ref, g_ref, o_ref, c, d):
        rows = pl.ds(pl.multiple_of(c * CHUNK, CHUNK), CHUNK)
        lhs, u, cdec = _delta_prep(q_ref[0, rows, :], k_ref[0, rows, :], v_ref[0, rows, :], g_ref[0, rows, :],
                                   d * DN_HEADS + h, (2 + d) * DN_HEADS + h, d, m_ref)
        s = s_scr[d]
        r = jnp.dot(lhs, jnp.concatenate([s.astype(BF16), u], axis=0), preferred_element_type=F32)
        s_scr[d] = jnp.concatenate([cdec] * (DN_DV // LANES), axis=1) * s + r[:DN_DK]
        o_ref[0, rows, :] = r[DN_DK:].astype(o_ref.dtype)

    def body(c, carry):
        one(qf_ref, kf_ref, vf_ref, gf_ref, of_ref, c, 0)
        one(qb_ref, kb_ref, vb_ref, gb_ref, ob_ref, nc - 1 - c, 1)
        return carry

    lax.fori_loop(0, nc, body, 0)

    @pl.when(step == pl.num_programs(2) - 1)
    def _():
        sfin_ref[0, 0] = s_scr[...]


def _delta_scan(qkv, gates, s0, masks):
    bsz, t, _ = qkv.shape
    tb = min(t, 1024)
    ns = t // tb
    vblk = DN_QK * 2 // DN_DV

    def fwd(width, off):
        return pl.BlockSpec((1, tb, width), lambda b, h, s: (b, s, off + h))

    def bwd(width, off):
        return pl.BlockSpec((1, tb, width), lambda b, h, s: (b, ns - 1 - s, off + h))

    state_spec = pl.BlockSpec((1, 1, 2, DN_DK, DN_DV), lambda b, h, s: (b, h, 0, 0, 0))
    o_shape = jax.ShapeDtypeStruct((bsz, t, DN_V), BF16)
    return pl.pallas_call(
        functools.partial(_delta_scan_kernel, nc=tb // CHUNK),
        grid=(bsz, DN_HEADS, ns),
        in_specs=[fwd(DN_DK, 0), fwd(DN_DK, DN_HEADS), fwd(DN_DV, vblk),
                  pl.BlockSpec((1, tb, LANES), lambda b, h, s: (b, s, 0)),
                  bwd(DN_DK, 0), bwd(DN_DK, DN_HEADS), bwd(DN_DV, vblk),
                  pl.BlockSpec((1, tb, LANES), lambda b, h, s: (b, ns - 1 - s, 0)),
                  state_spec,
                  pl.BlockSpec(masks.shape, lambda b, h, s: (0, 0, 0))],
        out_specs=[fwd(DN_DV, 0), bwd(DN_DV, 0), state_spec],
        out_shape=[o_shape, o_shape, jax.ShapeDtypeStruct(s0.shape, F32)],
        scratch_shapes=[pltpu.VMEM((2, DN_DK, DN_DV), F32)],
        compiler_params=_params(("parallel", "parallel", "arbitrary")),
        name="delta_scan",
    )(qkv, qkv, qkv, gates, qkv, qkv, qkv, gates, s0, masks)


def _gla_prep(q, k, lg, rev, m_ref):
    tri, off = m_ref[2 * rev], m_ref[2 * rev + 1]
    bc = _mm_f32(tri, lg)
    ref_rows = (SUB // 2 - 1, SUB + SUB // 2 - 1) if rev else (SUB // 2, SUB + SUB // 2)
    last_rows = (0, SUB) if rev else (SUB - 1, CHUNK - 1)

    def per_block(rows):
        return jnp.concatenate([jnp.broadcast_to(bc[r:r + 1], (SUB, GLA_DK)) for r in rows], axis=0)

    refb = per_block(ref_rows)
    lastb = per_block(last_rows)
    qf = q.astype(F32) * (GLA_DK ** -0.5)
    kf = k.astype(F32)
    qe = qf * jnp.exp(bc - refb)
    ke = kf * jnp.exp(refb - bc)
    qd = qf * jnp.exp(bc)
    kd = kf * jnp.exp(lastb - bc)
    a = _mm_nt(qe, ke) * tri + _mm_nt(qd, kd) * off
    last_lo = bc[last_rows[0]:last_rows[0] + 1]
    last_hi = bc[last_rows[1]:last_rows[1] + 1]
    last_p1, last_p2 = (last_hi, last_lo) if rev else (last_lo, last_hi)
    in_hi = lax.broadcasted_iota(jnp.int32, (CHUNK, GLA_DK), 0) >= SUB
    in_p2 = jnp.logical_not(in_hi) if rev else in_hi
    qd2 = qd * jnp.where(in_p2, jnp.exp(last_p1), 1.0)
    kd2 = kd * jnp.where(in_p2, 1.0, jnp.exp(last_p2))
    cdcol = jnp.broadcast_to(jnp.exp(last_p1 + last_p2), (GLA_DK, LANES)).T
    lhs = jnp.concatenate([qd2.astype(BF16), a.astype(BF16)], axis=1)
    return lhs, kd2.T.astype(BF16), cdcol


def _gla_scan_kernel(qf_ref, kf_ref, vf_ref, gf_ref, qb_ref, kb_ref, vb_ref, gb_ref, s0_ref, m_ref,
                     of_ref, ob_ref, sfin_ref, s_scr, *, nc):
    step = pl.program_id(2)

    @pl.when(step == 0)
    def _():
        s_scr[...] = s0_ref[0, 0]

    def one(q_ref, k_ref, v_ref, g_ref, o_ref, c, d):
        rows = pl.ds(pl.multiple_of(c * CHUNK, CHUNK), CHUNK)
        lhs, kdt, cdcol = _gla_prep(q_ref[0, rows, :], k_ref[0, rows, :], g_ref[0, rows, :], d, m_ref)
        v = v_ref[0, rows, :]
        s = s_scr[d]
        o = jnp.dot(lhs, jnp.concatenate([s.astype(BF16), v], axis=0), preferred_element_type=F32)
        o_ref[0, rows, :] = o.astype(o_ref.dtype)
        decay = jnp.concatenate([cdcol] * (GLA_DV // LANES), axis=1)
        s_scr[d] = decay * s + jnp.dot(kdt, v, preferred_element_type=F32)

    def body(c, carry):
        one(qf_ref, kf_ref, vf_ref, gf_ref, of_ref, c, 0)
        one(qb_ref, kb_ref, vb_ref, gb_ref, ob_ref, nc - 1 - c, 1)
        return carry

    lax.fori_loop(0, nc, body, 0)

    @pl.when(step == pl.num_programs(2) - 1)
    def _():
        sfin_ref[0, 0] = s_scr[...]


def _gla_scan(p, lg, s0, masks):
    bsz, t, _ = p.shape
    tb = min(t, 1024)
    ns = t // tb
    q_off = GLA_V // GLA_DK
    k_off = q_off + GLA_HEADS
    v_off = (GLA_V + 2 * GLA_K) // GLA_DV

    def spec(width, off, rev):
        if rev:
            return pl.BlockSpec((1, tb, width), lambda b, h, s: (b, ns - 1 - s, off + h))
        return pl.BlockSpec((1, tb, width), lambda b, h, s: (b, s, off + h))

    state_spec = pl.BlockSpec((1, 1, 2, GLA_DK, GLA_DV), lambda b, h, s: (b, h, 0, 0, 0))
    o_shape = jax.ShapeDtypeStruct((bsz, t, GLA_V), BF16)
    return pl.pallas_call(
        functools.partial(_gla_scan_kernel, nc=tb // CHUNK),
        grid=(bsz, GLA_HEADS, ns),
        in_specs=[spec(GLA_DK, q_off, 0), spec(GLA_DK, k_off, 0), spec(GLA_DV, v_off, 0), spec(GLA_DK, 0, 0),
                  spec(GLA_DK, q_off, 1), spec(GLA_DK, k_off, 1), spec(GLA_DV, v_off, 1),
                  spec(GLA_DK, GLA_HEADS, 1),
                  state_spec,
                  pl.BlockSpec(masks.shape, lambda b, h, s: (0, 0, 0))],
        out_specs=[spec(GLA_DV, 0, 0), spec(GLA_DV, 0, 1), state_spec],
        out_shape=[o_shape, o_shape, jax.ShapeDtypeStruct(s0.shape, F32)],
        scratch_shapes=[pltpu.VMEM((2, GLA_DK, GLA_DV), F32)],
        compiler_params=_params(("parallel", "parallel", "arbitrary")),
        name="gla_scan",
    )(p, p, p, lg, p, p, p, lg, s0, masks)


def _gated_out_kernel(of_ref, ob_ref, z_ref, x_ref, gt_ref, ng_ref, w_ref, fg_ref, o_ref, *, dv, final):
    o = of_ref[0].astype(F32) + ob_ref[0].astype(F32)
    parts = []
    for s in range(o.shape[1] // dv):
        seg = o[:, s * dv:(s + 1) * dv]
        ms = jnp.mean(seg * seg, axis=-1, keepdims=True)
        parts.append(seg * lax.rsqrt(ms + EPS))
    y = jnp.concatenate(parts, axis=1) * ng_ref[...] * _silu(z_ref[0].astype(F32))
    proj = jnp.dot(y.astype(BF16), w_ref[...], preferred_element_type=F32)
    xn = x_ref[0] + gt_ref[0] * proj
    if final:
        ms = jnp.mean(xn * xn, axis=-1, keepdims=True)
        xn = xn * lax.rsqrt(ms + EPS) * fg_ref[...]
    o_ref[0] = xn


def _gated_out(o_f, o_b, p, z_block, x, gt, norm_g_row, w_out, final_g_row, dv, final):
    bsz, t, d = x.shape
    dvt = o_f.shape[2]
    tm = min(t, 512)
    const = lambda b, i: (0, 0)
    return pl.pallas_call(
        functools.partial(_gated_out_kernel, dv=dv, final=final),
        grid=(bsz, t // tm),
        in_specs=[pl.BlockSpec((1, tm, dvt), lambda b, i: (b, i, 0)),
                  pl.BlockSpec((1, tm, dvt), lambda b, i: (b, i, 0)),
                  pl.BlockSpec((1, tm, dvt), lambda b, i: (b, i, z_block)),
                  pl.BlockSpec((1, tm, d), lambda b, i: (b, i, 0)),
                  pl.BlockSpec((1, 1, d), lambda b, i: (b, 0, 0)),
                  pl.BlockSpec((1, dvt), const),
                  pl.BlockSpec((dvt, d), const),
                  pl.BlockSpec((1, d), const)],
        out_specs=pl.BlockSpec((1, tm, d), lambda b, i: (b, i, 0)),
        out_shape=jax.ShapeDtypeStruct((bsz, t, d), F32),
        compiler_params=_params(("parallel", "parallel")),
        name="gated_out_final" if final else "gated_out",
    )(o_f, o_b, p, x, gt, norm_g_row, w_out, final_g_row)


def _pad_cols(w, width):
    return jnp.pad(w, ((0, 0), (0, width - w.shape[1])))


def kernel(x, c, ctx, c_ctx, mod_w, mod_b, norm_g, dn_w_in, dn_conv_w, dn_a_log, dn_dt_bias, dn_norm_g,
           dn_w_out, gla_w_in, gla_w_g2, gla_b_g, gla_norm_g, gla_w_out, final_g):
    bsz, _, d = x.shape
    assert bsz + 1 <= 8
    cond = jnp.concatenate([c, c_ctx[None], jnp.zeros((8 - bsz - 1, d), F32)], axis=0)
    mod = _modulation(cond, mod_w, mod_b)

    def mod_rows(layer):
        lat = [mod[layer, :bsz, None, n * d:(n + 1) * d] for n in range(3)]
        cx = [jnp.broadcast_to(mod[layer, bsz, n * d:(n + 1) * d], (bsz, 1, d)) for n in range(3)]
        return lat, cx

    final_row = final_g.reshape(1, d)

    (sh_l, sc_l, gt_l), (sh_c, sc_c, gt_c) = mod_rows(0)
    w0 = dn_w_in[0].astype(BF16)
    w0_main, w0_small = w0[:, :DN_MAIN], _pad_cols(w0[:, DN_MAIN:], LANES)
    gate_params = jnp.zeros((8, LANES), F32)
    gate_params = gate_params.at[0, :2 * DN_HEADS].set(dn_a_log[0].reshape(-1))
    gate_params = gate_params.at[1, :2 * DN_HEADS].set(dn_dt_bias[0].reshape(-1))
    g0 = norm_g[0].reshape(1, d)
    conv_w9 = dn_conv_w[0].reshape(9, DN_QKV)
    dmasks = _delta_mask_array()
    dn_g_row = jnp.tile(dn_norm_g[0], DN_HEADS).reshape(1, DN_V)
    dn_wo = dn_w_out[0].astype(BF16)

    def dn_features(u, sh, sc, on_grid):
        p, gates = _inproj(_dn_inproj_kernel, "dn_inproj", u, sh, sc, g0, w0_main, w0_small, [gate_params], LANES)
        return p, gates, _conv(p, conv_w9, on_grid)

    p_c, gates_c, qkv_c = dn_features(ctx, sh_c, sc_c, False)
    p_l, gates_l, qkv_l = dn_features(x, sh_l, sc_l, True)
    s_zero = jnp.zeros((bsz, DN_HEADS, 2, DN_DK, DN_DV), F32)
    oc_f, oc_b, s_ctx = _delta_scan(qkv_c, gates_c, s_zero, dmasks)
    ol_f, ol_b, _ = _delta_scan(qkv_l, gates_l, s_ctx, dmasks)
    z_block = DN_QKV // DN_V
    ctx1 = _gated_out(oc_f, oc_b, p_c, z_block, ctx, gt_c, dn_g_row, dn_wo, final_row, DN_DV, False)
    x1 = _gated_out(ol_f, ol_b, p_l, z_block, x, gt_l, dn_g_row, dn_wo, final_row, DN_DV, False)

    (sh_l, sc_l, gt_l), (sh_c, sc_c, _) = mod_rows(1)
    w1 = gla_w_in[0].astype(BF16)
    w1_main = jnp.concatenate([w1[:, 2 * GLA_K + GLA_V:GLA_MAIN], w1[:, :2 * GLA_K + GLA_V]], axis=1)
    w1_small = _pad_cols(w1[:, GLA_MAIN:], LANES)
    wg = jnp.zeros((LANES, 2 * GLA_K), F32)
    wg = wg.at[:GLA_RANK, :GLA_K].set(gla_w_g2[0, 0]).at[GLA_RANK:2 * GLA_RANK, GLA_K:].set(gla_w_g2[0, 1])
    wg = wg.astype(BF16)
    bg = gla_b_g[0].reshape(1, 2 * GLA_K)
    g1 = norm_g[1].reshape(1, d)
    gmasks = _gla_mask_array()
    gla_g_row = jnp.tile(gla_norm_g[0], GLA_HEADS).reshape(1, GLA_V)
    gla_wo = gla_w_out[0].astype(BF16)

    def gla_features(u, sh, sc):
        return _inproj(_gla_inproj_kernel, "gla_inproj", u, sh, sc, g1, w1_main, w1_small, [wg, bg], 2 * GLA_K)

    pg_c, lg_c = gla_features(ctx1, sh_c, sc_c)
    pg_l, lg_l = gla_features(x1, sh_l, sc_l)
    s_zero = jnp.zeros((bsz, GLA_HEADS, 2, GLA_DK, GLA_DV), F32)
    _, _, s_ctx = _gla_scan(pg_c, lg_c, s_zero, gmasks)
    og_f, og_b, _ = _gla_scan(pg_l, lg_l, s_ctx, gmasks)
    return _gated_out(og_f, og_b, pg_l, 0, x1, gt_l, gla_g_row, gla_wo, final_row, GLA_DV, True)
```

```python
import functools

import numpy as np
import jax
import jax.numpy as jnp
from jax import lax
from jax.experimental import pallas as pl
from jax.experimental.pallas import tpu as pltpu

F32, BF16 = jnp.float32, jnp.bfloat16
HIGHEST = lax.Precision.HIGHEST
EPS = 1e-6

LANES = 128
SUBLANES = 8
GRID_W = 64
CHUNK = 128
SUB = 64
SCAN_PAR = 4
VMEM_LIMIT_BYTES = 52 * 1024 * 1024

DN_DK, DN_HEADS, DN_DV = 128, 8, 256
DN_QK = DN_HEADS * DN_DK
DN_V = DN_HEADS * DN_DV
DN_QKV = 2 * DN_QK + DN_V
DN_MAIN = DN_QKV + DN_V
GLA_HEADS, GLA_DK, GLA_DV, GLA_RANK, GLA_TAU = 4, 128, 512, 16, 16.0
GLA_K = GLA_HEADS * GLA_DK
GLA_V = GLA_HEADS * GLA_DV
GLA_MAIN = 2 * GLA_K + 2 * GLA_V


def _sigmoid(x):
    return 1.0 / (1.0 + jnp.exp(-x))


def _silu(x):
    return x * _sigmoid(x)


def _softplus(x):
    return jnp.maximum(x, 0.0) + jnp.log(1.0 + jnp.exp(-jnp.abs(x)))


def _mm(a, b):
    return jnp.dot(a.astype(BF16), b.astype(BF16), preferred_element_type=F32)


def _mm_nt(a, b):
    return lax.dot_general(a.astype(BF16), b.astype(BF16), (((1,), (1,)), ((), ())),
                           preferred_element_type=F32)


def _mm_f32(a, b):
    return jnp.dot(a, b, precision=HIGHEST, preferred_element_type=F32)


def _params(semantics):
    return pltpu.CompilerParams(dimension_semantics=semantics, vmem_limit_bytes=VMEM_LIMIT_BYTES)


def _mod_kernel(c_ref, w_ref, b_ref, o_ref):
    o_ref[0] = _mm_f32(_silu(c_ref[...]), w_ref[0]) + b_ref[0]


def _modulation(cond, mod_w, mod_b):
    depth, d, d3 = mod_w.shape
    tn = d
    return pl.pallas_call(
        _mod_kernel,
        grid=(depth, d3 // tn),
        in_specs=[pl.BlockSpec((8, d), lambda l, j: (0, 0)),
                  pl.BlockSpec((1, d, tn), lambda l, j: (l, 0, j)),
                  pl.BlockSpec((1, 1, tn), lambda l, j: (l, 0, j))],
        out_specs=pl.BlockSpec((1, 8, tn), lambda l, j: (l, 0, j)),
        out_shape=jax.ShapeDtypeStruct((depth, 8, d3), F32),
        compiler_params=_params(("parallel", "parallel")),
        name="modulation",
    )(cond, mod_w, mod_b.reshape(depth, 1, d3))


def _norm_mod(x, g, sc, sh):
    ms = jnp.mean(x * x, axis=-1, keepdims=True)
    return x * lax.rsqrt(ms + EPS) * g * (1.0 + sc) + sh


def _dn_inproj_kernel(x_ref, sh_ref, sc_ref, g_ref, w_ref, ws_ref, ap_ref, o_ref, gates_ref, gates_t_ref, h_scr):
    @pl.when(pl.program_id(2) == 0)
    def _():
        hb = _norm_mod(x_ref[0], g_ref[...], sc_ref[0], sh_ref[0]).astype(BF16)
        h_scr[...] = hb
        raw = jnp.dot(hb, ws_ref[...], preferred_element_type=F32)
        log_decay = -jnp.exp(ap_ref[0:1, :]) * _softplus(raw + ap_ref[1:2, :])
        beta = _sigmoid(raw)
        r = lax.broadcasted_iota(jnp.int32, (2 * CHUNK, CHUNK), 0)
        c = lax.broadcasted_iota(jnp.int32, (2 * CHUNK, CHUNK), 1)
        tri2 = (jnp.where(r < CHUNK, r - c, c - r + CHUNK) >= 0).astype(F32)
        lane = lax.broadcasted_iota(jnp.int32, (CHUNK, LANES), 1)
        for n in range(raw.shape[0] // CHUNK):
            rows = slice(n * CHUNK, (n + 1) * CHUNK)
            cs = _mm_f32(tri2, log_decay[rows])
            gt = jnp.where(lane < DN_HEADS, cs[:CHUNK], jnp.where(lane < 2 * DN_HEADS, cs[CHUNK:], beta[rows]))
            gates_ref[0, rows, :] = gt
            gtt = gt.T
            gates_t_ref[0, :, rows] = jnp.concatenate(
                [jnp.broadcast_to(gtt[m:m + 1], (SUBLANES, CHUNK)) for m in range(2 * DN_HEADS)], axis=0)

    o_ref[0] = jnp.dot(h_scr[...], w_ref[...], preferred_element_type=F32).astype(o_ref.dtype)


def _gla_inproj_kernel(x_ref, sh_ref, sc_ref, g_ref, w_ref, ws_ref, wg_ref, bg_ref, o_ref, lg_ref, h_scr):
    @pl.when(pl.program_id(2) == 0)
    def _():
        hb = _norm_mod(x_ref[0], g_ref[...], sc_ref[0], sh_ref[0]).astype(BF16)
        h_scr[...] = hb
        gl = jnp.dot(hb, ws_ref[...], preferred_element_type=F32)
        zg = jnp.dot(gl.astype(BF16), wg_ref[...], preferred_element_type=F32) + bg_ref[...]
        lg_ref[0] = -_softplus(-zg) * (1.0 / GLA_TAU)

    o_ref[0] = jnp.dot(h_scr[...], w_ref[...], preferred_element_type=F32).astype(o_ref.dtype)


def _inproj(kernel_fn, name, x, sh, sc, g, w_main, w_small, extra, aux_width, aux_transposed=False):
    bsz, t, d = x.shape
    n = w_main.shape[1]
    tm = min(t, 1024)
    tn = 512
    const = lambda b, i, j: (0, 0)
    extra_specs = [pl.BlockSpec(e.shape, const) for e in extra]
    out_specs = [pl.BlockSpec((1, tm, tn), lambda b, i, j: (b, i, j)),
                 pl.BlockSpec((1, tm, aux_width), lambda b, i, j: (b, i, 0))]
    out_shape = [jax.ShapeDtypeStruct((bsz, t, n), BF16), jax.ShapeDtypeStruct((bsz, t, aux_width), F32)]
    if aux_transposed:
        out_specs.append(pl.BlockSpec((1, aux_width, tm), lambda b, i, j: (b, 0, i)))
        out_shape.append(jax.ShapeDtypeStruct((bsz, aux_width, t), F32))
    return pl.pallas_call(
        kernel_fn,
        grid=(bsz, t // tm, n // tn),
        in_specs=[pl.BlockSpec((1, tm, d), lambda b, i, j: (b, i, 0)),
                  pl.BlockSpec((1, 1, d), lambda b, i, j: (b, 0, 0)),
                  pl.BlockSpec((1, 1, d), lambda b, i, j: (b, 0, 0)),
                  pl.BlockSpec((1, d), const),
                  pl.BlockSpec((d, tn), lambda b, i, j: (0, j)),
                  pl.BlockSpec((d, LANES), const)] + extra_specs,
        out_specs=out_specs,
        out_shape=out_shape,
        scratch_shapes=[pltpu.VMEM((tm, d), BF16)],
        compiler_params=_params(("parallel", "parallel", "arbitrary")),
        name=name,
    )(x, sh, sc, g, w_main, w_small, *extra)


def _conv_kernel(prev_ref, main_ref, next_ref, w_ref, o_ref, *, on_grid, n_q_tiles, n_qk_tiles):
    i = pl.program_id(1)
    j = pl.program_id(2)
    x = main_ref[0].astype(F32)
    tt, tc = x.shape
    w = w_ref[...]
    tok = lax.broadcasted_iota(jnp.int32, (tt, tc), 0)
    if on_grid:
        prev = jnp.where(i > 0, prev_ref[0].astype(F32), 0.0)
        nxt = jnp.where(i < pl.num_programs(1) - 1, next_ref[0].astype(F32), 0.0)
        ext = jnp.concatenate([prev, x, nxt], axis=0)
        rows = [ext[0:tt], x, ext[2 * GRID_W:2 * GRID_W + tt]]
        accs = [w[dc:dc + 1] * rows[0] + w[3 + dc:4 + dc] * rows[1] + w[6 + dc:7 + dc] * rows[2]
                for dc in range(3)]
        pos = tok & (GRID_W - 1)
        first, last = pos == 0, pos == GRID_W - 1
    else:
        accs = [w[3 + dc:4 + dc] * x for dc in range(3)]
        first, last = tok == 0, tok == tt - 1
    y = (accs[1] + jnp.where(first, 0.0, pltpu.roll(accs[0], 1, 0))
         + jnp.where(last, 0.0, pltpu.roll(accs[2], tt - 1, 0)))
    y = _silu(y)

    @pl.when(j < n_qk_tiles)
    def _():
        scale = jnp.where(j < n_q_tiles, DN_DK ** -0.5, 1.0)
        for s in range(tc // DN_DK):
            seg = y[:, s * DN_DK:(s + 1) * DN_DK]
            ss = jnp.sum(seg * seg, axis=-1, keepdims=True)
            o_ref[0, :, s * DN_DK:(s + 1) * DN_DK] = (seg * (lax.rsqrt(ss + EPS) * scale)).astype(o_ref.dtype)

    @pl.when(j >= n_qk_tiles)
    def _():
        o_ref[0] = y.astype(o_ref.dtype)


def _conv(p, conv_w9, on_grid):
    bsz, t, _ = p.shape
    tc = 512
    tt = min(t, 512) if on_grid else t
    rows_per_tile = tt // GRID_W
    n_rows = t // GRID_W
    kern = functools.partial(_conv_kernel, on_grid=on_grid, n_q_tiles=DN_QK // tc, n_qk_tiles=2 * DN_QK // tc)
    return pl.pallas_call(
        kern,
        grid=(bsz, t // tt, DN_QKV // tc),
        in_specs=[pl.BlockSpec((1, GRID_W, tc), lambda b, i, j: (b, jnp.maximum(i * rows_per_tile - 1, 0), j)),
                  pl.BlockSpec((1, tt, tc), lambda b, i, j: (b, i, j)),
                  pl.BlockSpec((1, GRID_W, tc),
                               lambda b, i, j: (b, jnp.minimum((i + 1) * rows_per_tile, n_rows - 1), j)),
                  pl.BlockSpec((9, tc), lambda b, i, j: (0, j))],
        out_specs=pl.BlockSpec((1, tt, tc), lambda b, i, j: (b, i, j)),
        out_shape=jax.ShapeDtypeStruct((bsz, t, DN_QKV), BF16),
        compiler_params=_params(("parallel", "parallel", "parallel")),
        name="conv_grid" if on_grid else "conv_seq",
    )(p, p, p, conv_w9)


_BASE_BLOCK = 8
_M_EYE, _M_DIAG, _M_MERGE = 4, 5, 6


def _delta_mask_array():
    r = np.arange(CHUNK)[:, None]
    c = np.arange(CHUNK)[None, :]
    mats = [r >= c, r > c, r <= c, r < c, r == c, (r // _BASE_BLOCK) == (c // _BASE_BLOCK)]
    b = _BASE_BLOCK
    while b < CHUNK:
        mats.append(((r // (2 * b)) == (c // (2 * b))) & ((r // b) != (c // b)))
        b *= 2
    return jnp.asarray(np.stack(mats).astype(np.float32))


def _gla_mask_array():
    r = np.arange(CHUNK)[:, None]
    c = np.arange(CHUNK)[None, :]
    same = (r // SUB) == (c // SUB)
    mats = [(r >= c) & same, (r // SUB > c // SUB), (r <= c) & same, (r // SUB < c // SUB)]
    return jnp.asarray(np.stack(mats).astype(np.float32))


def _delta_prep_group(qs, ks, vs, gates, gc_rows, revs, h, m_ref):
    lane = lax.broadcasted_iota(jnp.int32, (CHUNK, LANES), 1)

    def pick(gt, idx):
        return jnp.broadcast_to(jnp.sum(jnp.where(lane == idx, gt, 0.0), axis=1, keepdims=True), (CHUNK, LANES))

    gc = [pick(gt, rev * DN_HEADS + h) for gt, rev in zip(gates, revs)]
    beta = [pick(gt, (2 + rev) * DN_HEADS + h) for gt, rev in zip(gates, revs)]
    decay = [jnp.exp(jnp.minimum(x - y, 0.0)) for x, y in zip(gc, gc_rows)]
    kq = [_mm_nt(jnp.concatenate([k, q], axis=0), k) for q, k in zip(qs, ks)]
    lm = [x[:CHUNK] * dec * m_ref[2 * rev + 1] * bt for x, dec, rev, bt in zip(kq, decay, revs, beta)]
    a = [(x[CHUNK:] * dec * m_ref[2 * rev]).astype(BF16) for x, dec, rev in zip(kq, decay, revs)]

    lb = [x * m_ref[_M_DIAG] for x in lm]
    l2 = [_mm(x, x) for x in lb]
    l4 = [_mm(x, x) for x in l2]
    inv = [m_ref[_M_EYE] - x for x in lb]
    inv = [x + _mm(x, y) for x, y in zip(inv, l2)]
    inv = [x + _mm(x, y) for x, y in zip(inv, l4)]
    for level in range(_M_MERGE, m_ref.shape[0]):
        xc = [_mm(x, y * m_ref[level]) for x, y in zip(inv, lm)]
        inv = [x - _mm(y, x) for x, y in zip(inv, xc)]

    kf = [k.astype(F32) for k in ks]
    e_pos = [jnp.exp(x) for x in gc]
    ends = [x[0:1, :] if rev else x[CHUNK - 1:CHUNK, :] for x, rev in zip(gc, revs)]
    rhs = [jnp.concatenate([(k * (bt * e)).astype(BF16),
                            (v.astype(F32) * jnp.concatenate([bt] * (DN_DV // LANES), axis=1)).astype(BF16)], axis=1)
           for k, v, bt, e in zip(kf, vs, beta, e_pos)]
    wu = [jnp.dot(x.astype(BF16), y, preferred_element_type=F32) for x, y in zip(inv, rhs)]
    w = [x[:, :DN_DK].astype(BF16) for x in wu]
    u = [x[:, DN_DK:].astype(BF16) for x in wu]
    kdt = [(k * jnp.exp(end - x)).T.astype(BF16) for k, end, x in zip(kf, ends, gc)]
    aw = [jnp.dot(jnp.concatenate([x, y], axis=0), z, preferred_element_type=F32) for x, y, z in zip(kdt, a, w)]
    lhs = [jnp.concatenate([jnp.concatenate([(-x[:DN_DK]).astype(BF16), y], axis=1),
                            jnp.concatenate([(q.astype(F32) * e - x[DN_DK:]).astype(BF16), z], axis=1)], axis=0)
           for x, y, z, q, e in zip(aw, kdt, a, qs, e_pos)]
    cdec = [jnp.concatenate([jnp.exp(end)] * (DN_DV // LANES), axis=1) for end in ends]
    return lhs, u, cdec


def _delta_scan_kernel(qf_ref, kf_ref, vf_ref, gf_ref, gtf_ref, qb_ref, kb_ref, vb_ref, gb_ref, gtb_ref,
                       s0_ref, m_ref, of_ref, ob_ref, sfin_ref, s_scr, *, nc, par):
    h = pl.program_id(1)
    step = pl.program_id(2)

    @pl.when(step == 0)
    def _():
        s_scr[...] = s0_ref[0, 0]

    def body(it, carry):
        rows = ([pl.ds(pl.multiple_of((it * par + p) * CHUNK, CHUNK), CHUNK) for p in range(par)]
                + [pl.ds(pl.multiple_of((nc - 1 - it * par - p) * CHUNK, CHUNK), CHUNK) for p in range(par)])
        refs = ([(qf_ref, kf_ref, vf_ref, gf_ref, of_ref, gtf_ref)] * par
                + [(qb_ref, kb_ref, vb_ref, gb_ref, ob_ref, gtb_ref)] * par)
        revs = [0] * par + [1] * par
        gc_rows = [r[5][0, pl.ds(pl.multiple_of((rev * DN_HEADS + h) * SUBLANES, SUBLANES), SUBLANES), rw][0:1]
                   for r, rw, rev in zip(refs, rows, revs)]
        lhs, u, cdec = _delta_prep_group([r[0][0, rw, :] for r, rw in zip(refs, rows)],
                                         [r[1][0, rw, :] for r, rw in zip(refs, rows)],
                                         [r[2][0, rw, :] for r, rw in zip(refs, rows)],
                                         [r[3][0, rw, :] for r, rw in zip(refs, rows)], gc_rows, revs, h, m_ref)
        state = [s_scr[0], s_scr[1]]
        for p in range(par):
            idx = [p, par + p]
            res = [jnp.dot(lhs[n], jnp.concatenate([state[d].astype(BF16), u[n]], axis=0),
                           preferred_element_type=F32) for d, n in enumerate(idx)]
            state = [cdec[n] * state[d] + res[d][:DN_DK] for d, n in enumerate(idx)]
            for d, n in enumerate(idx):
                refs[n][4][0, rows[n], :] = res[d][DN_DK:].astype(refs[n][4].dtype)
        s_scr[0] = state[0]
        s_scr[1] = state[1]
        return carry

    lax.fori_loop(0, nc // par, body, 0)

    @pl.when(step == pl.num_programs(2) - 1)
    def _():
        sfin_ref[0, 0] = s_scr[...]


def _delta_scan(qkv, gates, gates_t, s0, masks):
    bsz, t, _ = qkv.shape
    tb = min(t, 1024)
    ns = t // tb
    vblk = DN_QK * 2 // DN_DV

    def fwd(width, off):
        return pl.BlockSpec((1, tb, width), lambda b, h, s: (b, s, off + h))

    def bwd(width, off):
        return pl.BlockSpec((1, tb, width), lambda b, h, s: (b, ns - 1 - s, off + h))

    state_spec = pl.BlockSpec((1, 1, 2, DN_DK, DN_DV), lambda b, h, s: (b, h, 0, 0, 0))
    o_shape = jax.ShapeDtypeStruct((bsz, t, DN_V), BF16)
    return pl.pallas_call(
        functools.partial(_delta_scan_kernel, nc=tb // CHUNK, par=min(SCAN_PAR, tb // CHUNK)),
        grid=(bsz, DN_HEADS, ns),
        in_specs=[fwd(DN_DK, 0), fwd(DN_DK, DN_HEADS), fwd(DN_DV, vblk),
                  pl.BlockSpec((1, tb, LANES), lambda b, h, s: (b, s, 0)),
                  pl.BlockSpec((1, LANES, tb), lambda b, h, s: (b, 0, s)),
                  bwd(DN_DK, 0), bwd(DN_DK, DN_HEADS), bwd(DN_DV, vblk),
                  pl.BlockSpec((1, tb, LANES), lambda b, h, s: (b, ns - 1 - s, 0)),
                  pl.BlockSpec((1, LANES, tb), lambda b, h, s: (b, 0, ns - 1 - s)),
                  state_spec,
                  pl.BlockSpec(masks.shape, lambda b, h, s: (0, 0, 0))],
        out_specs=[fwd(DN_DV, 0), bwd(DN_DV, 0), state_spec],
        out_shape=[o_shape, o_shape, jax.ShapeDtypeStruct(s0.shape, F32)],
        scratch_shapes=[pltpu.VMEM((2, DN_DK, DN_DV), F32)],
        compiler_params=_params(("parallel", "parallel", "arbitrary")),
        name="delta_scan",
    )(qkv, qkv, qkv, gates, gates_t, qkv, qkv, qkv, gates, gates_t, s0, masks)


def _gla_prep_group(qs, ks, lgs, revs, m_ref):
    in_hi = lax.broadcasted_iota(jnp.int32, (CHUNK, GLA_DK), 0) >= SUB
    bc = [_mm_f32(m_ref[2 * rev], lg) for lg, rev in zip(lgs, revs)]

    def per_block(x, rows):
        return jnp.concatenate([jnp.broadcast_to(x[r:r + 1], (SUB, GLA_DK)) for r in rows], axis=0)

    ref_rows = [(SUB // 2 - 1, SUB + SUB // 2 - 1) if rev else (SUB // 2, SUB + SUB // 2) for rev in revs]
    last_rows = [(0, SUB) if rev else (SUB - 1, CHUNK - 1) for rev in revs]
    refb = [per_block(x, rows) for x, rows in zip(bc, ref_rows)]
    lastb = [per_block(x, rows) for x, rows in zip(bc, last_rows)]
    qf = [q.astype(F32) * (GLA_DK ** -0.5) for q in qs]
    kf = [k.astype(F32) for k in ks]
    qe = [q * jnp.exp(x - r) for q, x, r in zip(qf, bc, refb)]
    ke = [k * jnp.exp(r - x) for k, x, r in zip(kf, bc, refb)]
    qd = [q * jnp.exp(x) for q, x in zip(qf, bc)]
    kd = [k * jnp.exp(l - x) for k, x, l in zip(kf, bc, lastb)]
    a_in = [_mm_nt(x, y) for x, y in zip(qe, ke)]
    a_x = [_mm_nt(x, y) for x, y in zip(qd, kd)]
    a = [x * m_ref[2 * rev] + y * m_ref[2 * rev + 1] for x, y, rev in zip(a_in, a_x, revs)]
    last1 = [x[rows[1]:rows[1] + 1] if rev else x[rows[0]:rows[0] + 1] for x, rows, rev in zip(bc, last_rows, revs)]
    last2 = [x[rows[0]:rows[0] + 1] if rev else x[rows[1]:rows[1] + 1] for x, rows, rev in zip(bc, last_rows, revs)]
    second = [jnp.logical_not(in_hi) if rev else in_hi for rev in revs]
    qd2 = [x * jnp.where(sec, jnp.exp(l1), 1.0) for x, sec, l1 in zip(qd, second, last1)]
    kd2 = [x * jnp.where(sec, 1.0, jnp.exp(l2)) for x, sec, l2 in zip(kd, second, last2)]
    cdcol = [jnp.broadcast_to(jnp.exp(l1 + l2), (GLA_DK, LANES)).T for l1, l2 in zip(last1, last2)]
    lhs = [jnp.concatenate([x.astype(BF16), y.astype(BF16)], axis=1) for x, y in zip(qd2, a)]
    kdt = [x.T.astype(BF16) for x in kd2]
    decay = [jnp.concatenate([x] * (GLA_DV // LANES), axis=1) for x in cdcol]
    return lhs, kdt, decay


def _gla_scan_kernel(qf_ref, kf_ref, vf_ref, gf_ref, qb_ref, kb_ref, vb_ref, gb_ref, s0_ref, m_ref,
                     of_ref, ob_ref, sfin_ref, s_scr, *, nc, par):
    step = pl.program_id(2)

    @pl.when(step == 0)
    def _():
        s_scr[...] = s0_ref[0, 0]

    def body(it, carry):
        rows = ([pl.ds(pl.multiple_of((it * par + p) * CHUNK, CHUNK), CHUNK) for p in range(par)]
                + [pl.ds(pl.multiple_of((nc - 1 - it * par - p) * CHUNK, CHUNK), CHUNK) for p in range(par)])
        refs = [(qf_ref, kf_ref, vf_ref, gf_ref, of_ref)] * par + [(qb_ref, kb_ref, vb_ref, gb_ref, ob_ref)] * par
        revs = [0] * par + [1] * par
        lhs, kdt, decay = _gla_prep_group([r[0][0, rw, :] for r, rw in zip(refs, rows)],
                                          [r[1][0, rw, :] for r, rw in zip(refs, rows)],
                                          [r[3][0, rw, :] for r, rw in zip(refs, rows)], revs, m_ref)
        vs = [r[2][0, rw, :] for r, rw in zip(refs, rows)]
        upd = [jnp.dot(x, v, preferred_element_type=F32) for x, v in zip(kdt, vs)]
        state = [s_scr[0], s_scr[1]]
        for p in range(par):
            idx = [p, par + p]
            outs = [jnp.dot(lhs[n], jnp.concatenate([state[d].astype(BF16), vs[n]], axis=0),
                            preferred_element_type=F32) for d, n in enumerate(idx)]
            state = [decay[n] * state[d] + upd[n] for d, n in enumerate(idx)]
            for d, n in enumerate(idx):
                refs[n][4][0, rows[n], :] = outs[d].astype(refs[n][4].dtype)
        s_scr[0] = state[0]
        s_scr[1] = state[1]
        return carry

    lax.fori_loop(0, nc // par, body, 0)

    @pl.when(step == pl.num_programs(2) - 1)
    def _():
        sfin_ref[0, 0] = s_scr[...]


def _gla_scan(p, lg, s0, masks):
    bsz, t, _ = p.shape
    tb = min(t, 1024)
    ns = t // tb
    q_off = GLA_V // GLA_DK
    k_off = q_off + GLA_HEADS
    v_off = (GLA_V + 2 * GLA_K) // GLA_DV

    def spec(width, off, rev):
        if rev:
            return pl.BlockSpec((1, tb, width), lambda b, h, s: (b, ns - 1 - s, off + h))
        return pl.BlockSpec((1, tb, width), lambda b, h, s: (b, s, off + h))

    state_spec = pl.BlockSpec((1, 1, 2, GLA_DK, GLA_DV), lambda b, h, s: (b, h, 0, 0, 0))
    o_shape = jax.ShapeDtypeStruct((bsz, t, GLA_V), BF16)
    return pl.pallas_call(
        functools.partial(_gla_scan_kernel, nc=tb // CHUNK, par=min(SCAN_PAR, tb // CHUNK)),
        grid=(bsz, GLA_HEADS, ns),
        in_specs=[spec(GLA_DK, q_off, 0), spec(GLA_DK, k_off, 0), spec(GLA_DV, v_off, 0), spec(GLA_DK, 0, 0),
                  spec(GLA_DK, q_off, 1), spec(GLA_DK, k_off, 1), spec(GLA_DV, v_off, 1),
                  spec(GLA_DK, GLA_HEADS, 1),
                  state_spec,
                  pl.BlockSpec(masks.shape, lambda b, h, s: (0, 0, 0))],
        out_specs=[spec(GLA_DV, 0, 0), spec(GLA_DV, 0, 1), state_spec],
        out_shape=[o_shape, o_shape, jax.ShapeDtypeStruct(s0.shape, F32)],
        scratch_shapes=[pltpu.VMEM((2, GLA_DK, GLA_DV), F32)],
        compiler_params=_params(("parallel", "parallel", "arbitrary")),
        name="gla_scan",
    )(p, p, p, lg, p, p, p, lg, s0, masks)


def _gated_out_kernel(of_ref, ob_ref, z_ref, x_ref, gt_ref, ng_ref, w_ref, fg_ref, o_ref, *, dv, final):
    o = of_ref[0].astype(F32) + ob_ref[0].astype(F32)
    parts = []
    for s in range(o.shape[1] // dv):
        seg = o[:, s * dv:(s + 1) * dv]
        ms = jnp.mean(seg * seg, axis=-1, keepdims=True)
        parts.append(seg * lax.rsqrt(ms + EPS))
    y = jnp.concatenate(parts, axis=1) * ng_ref[...] * _silu(z_ref[0].astype(F32))
    proj = jnp.dot(y.astype(BF16), w_ref[...], preferred_element_type=F32)
    xn = x_ref[0] + gt_ref[0] * proj
    if final:
        ms = jnp.mean(xn * xn, axis=-1, keepdims=True)
        xn = xn * lax.rsqrt(ms + EPS) * fg_ref[...]
    o_ref[0] = xn


def _gated_out(o_f, o_b, p, z_block, x, gt, norm_g_row, w_out, final_g_row, dv, final):
    bsz, t, d = x.shape
    dvt = o_f.shape[2]
    tm = min(t, 512)
    const = lambda b, i: (0, 0)
    return pl.pallas_call(
        functools.partial(_gated_out_kernel, dv=dv, final=final),
        grid=(bsz, t // tm),
        in_specs=[pl.BlockSpec((1, tm, dvt), lambda b, i: (b, i, 0)),
                  pl.BlockSpec((1, tm, dvt), lambda b, i: (b, i, 0)),
                  pl.BlockSpec((1, tm, dvt), lambda b, i: (b, i, z_block)),
                  pl.BlockSpec((1, tm, d), lambda b, i: (b, i, 0)),
                  pl.BlockSpec((1, 1, d), lambda b, i: (b, 0, 0)),
                  pl.BlockSpec((1, dvt), const),
                  pl.BlockSpec((dvt, d), const),
                  pl.BlockSpec((1, d), const)],
        out_specs=pl.BlockSpec((1, tm, d), lambda b, i: (b, i, 0)),
        out_shape=jax.ShapeDtypeStruct((bsz, t, d), F32),
        compiler_params=_params(("parallel", "parallel")),
        name="gated_out_final" if final else "gated_out",
    )(o_f, o_b, p, x, gt, norm_g_row, w_out, final_g_row)


def _pad_cols(w, width):
    return jnp.pad(w, ((0, 0), (0, width - w.shape[1])))


def kernel(x, c, ctx, c_ctx, mod_w, mod_b, norm_g, dn_w_in, dn_conv_w, dn_a_log, dn_dt_bias, dn_norm_g,
           dn_w_out, gla_w_in, gla_w_g2, gla_b_g, gla_norm_g, gla_w_out, final_g):
    bsz, _, d = x.shape
    assert bsz + 1 <= 8
    cond = jnp.concatenate([c, c_ctx[None], jnp.zeros((8 - bsz - 1, d), F32)], axis=0)
    mod = _modulation(cond, mod_w, mod_b)

    def mod_rows(layer):
        lat = [mod[layer, :bsz, None, n * d:(n + 1) * d] for n in range(3)]
        cx = [jnp.broadcast_to(mod[layer, bsz, n * d:(n + 1) * d], (bsz, 1, d)) for n in range(3)]
        return lat, cx

    final_row = final_g.reshape(1, d)

    (sh_l, sc_l, gt_l), (sh_c, sc_c, gt_c) = mod_rows(0)
    w0 = dn_w_in[0].astype(BF16)
    w0_main, w0_small = w0[:, :DN_MAIN], _pad_cols(w0[:, DN_MAIN:], LANES)
    gate_params = jnp.zeros((8, LANES), F32)
    gate_params = gate_params.at[0, :2 * DN_HEADS].set(dn_a_log[0].reshape(-1))
    gate_params = gate_params.at[1, :2 * DN_HEADS].set(dn_dt_bias[0].reshape(-1))
    g0 = norm_g[0].reshape(1, d)
    conv_w9 = dn_conv_w[0].reshape(9, DN_QKV)
    dmasks = _delta_mask_array()
    dn_g_row = jnp.tile(dn_norm_g[0], DN_HEADS).reshape(1, DN_V)
    dn_wo = dn_w_out[0].astype(BF16)

    def dn_features(u, sh, sc, on_grid):
        p, gates, gates_t = _inproj(_dn_inproj_kernel, "dn_inproj", u, sh, sc, g0, w0_main, w0_small,
                                    [gate_params], LANES, aux_transposed=True)
        return p, gates, gates_t, _conv(p, conv_w9, on_grid)

    p_c, gates_c, gates_tc, qkv_c = dn_features(ctx, sh_c, sc_c, False)
    p_l, gates_l, gates_tl, qkv_l = dn_features(x, sh_l, sc_l, True)
    s_zero = jnp.zeros((bsz, DN_HEADS, 2, DN_DK, DN_DV), F32)
    oc_f, oc_b, s_ctx = _delta_scan(qkv_c, gates_c, gates_tc, s_zero, dmasks)
    ol_f, ol_b, _ = _delta_scan(qkv_l, gates_l, gates_tl, s_ctx, dmasks)
    z_block = DN_QKV // DN_V
    ctx1 = _gated_out(oc_f, oc_b, p_c, z_block, ctx, gt_c, dn_g_row, dn_wo, final_row, DN_DV, False)
    x1 = _gated_out(ol_f, ol_b, p_l, z_block, x, gt_l, dn_g_row, dn_wo, final_row, DN_DV, False)

    (sh_l, sc_l, gt_l), (sh_c, sc_c, _) = mod_rows(1)
    w1 = gla_w_in[0].astype(BF16)
    w1_main = jnp.concatenate([w1[:, 2 * GLA_K + GLA_V:GLA_MAIN], w1[:, :2 * GLA_K + GLA_V]], axis=1)
    w1_small = _pad_cols(w1[:, GLA_MAIN:], LANES)
    wg = jnp.zeros((LANES, 2 * GLA_K), F32)
    wg = wg.at[:GLA_RANK, :GLA_K].set(gla_w_g2[0, 0]).at[GLA_RANK:2 * GLA_RANK, GLA_K:].set(gla_w_g2[0, 1])
    wg = wg.astype(BF16)
    bg = gla_b_g[0].reshape(1, 2 * GLA_K)
    g1 = norm_g[1].reshape(1, d)
    gmasks = _gla_mask_array()
    gla_g_row = jnp.tile(gla_norm_g[0], GLA_HEADS).reshape(1, GLA_V)
    gla_wo = gla_w_out[0].astype(BF16)

    def gla_features(u, sh, sc):
        return _inproj(_gla_inproj_kernel, "gla_inproj", u, sh, sc, g1, w1_main, w1_small, [wg, bg], 2 * GLA_K)

    pg_c, lg_c = gla_features(ctx1, sh_c, sc_c)
    pg_l, lg_l = gla_features(x1, sh_l, sc_l)
    s_zero = jnp.zeros((bsz, GLA_HEADS, 2, GLA_DK, GLA_DV), F32)
    _, _, s_ctx = _gla_scan(pg_c, lg_c, s_zero, gmasks)
    og_f, og_b, _ = _gla_scan(pg_l, lg_l, s_ctx, gmasks)
    return _gated_out(og_f, og_b, pg_l, 0, x1, gt_l, gla_g_row, gla_wo, final_row, GLA_DV, True)
```

```python
import functools

import numpy as np
import jax
import jax.numpy as jnp
from jax import lax
from jax.experimental import pallas as pl
from jax.experimental.pallas import tpu as pltpu

F32, BF16 = jnp.float32, jnp.bfloat16
HIGHEST = lax.Precision.HIGHEST
EPS = 1e-6

LANES = 128
SUBLANES = 8
GRID_W = 64
CHUNK = 128
SUB = 64
SCAN_PAR = 4
DELTA_HEADS_PER_STEP = 2
VMEM_LIMIT_BYTES = 52 * 1024 * 1024

DN_DK, DN_HEADS, DN_DV = 128, 8, 256
DN_QK = DN_HEADS * DN_DK
DN_V = DN_HEADS * DN_DV
DN_QKV = 2 * DN_QK + DN_V
DN_MAIN = DN_QKV + DN_V
GLA_HEADS, GLA_DK, GLA_DV, GLA_RANK, GLA_TAU = 4, 128, 512, 16, 16.0
GLA_K = GLA_HEADS * GLA_DK
GLA_V = GLA_HEADS * GLA_DV
GLA_MAIN = 2 * GLA_K + 2 * GLA_V


def _sigmoid(x):
    return 1.0 / (1.0 + jnp.exp(-x))


def _silu(x):
    return x * _sigmoid(x)


def _softplus(x):
    return jnp.maximum(x, 0.0) + jnp.log(1.0 + jnp.exp(-jnp.abs(x)))


def _mm(a, b):
    return jnp.dot(a.astype(BF16), b.astype(BF16), preferred_element_type=F32)


def _mm_nt(a, b):
    return lax.dot_general(a.astype(BF16), b.astype(BF16), (((1,), (1,)), ((), ())),
                           preferred_element_type=F32)


def _masked_sums(mask01, x):
    hi = x.astype(BF16)
    r1 = x - hi.astype(F32)
    mid = r1.astype(BF16)
    lo = (r1 - mid.astype(F32)).astype(BF16)
    n = x.shape[1]
    s = jnp.dot(mask01.astype(BF16), jnp.concatenate([hi, mid, lo], axis=1), preferred_element_type=F32)
    return (s[:, :n] + s[:, n:2 * n]) + s[:, 2 * n:]


def _mm_f32(a, b):
    return jnp.dot(a, b, precision=HIGHEST, preferred_element_type=F32)


def _params(semantics):
    return pltpu.CompilerParams(dimension_semantics=semantics, vmem_limit_bytes=VMEM_LIMIT_BYTES)


def _mod_kernel(c_ref, w_ref, b_ref, o_ref):
    o_ref[0] = _mm_f32(_silu(c_ref[...]), w_ref[0]) + b_ref[0]


def _modulation(cond, mod_w, mod_b):
    depth, d, d3 = mod_w.shape
    tn = d
    return pl.pallas_call(
        _mod_kernel,
        grid=(depth, d3 // tn),
        in_specs=[pl.BlockSpec((8, d), lambda l, j: (0, 0)),
                  pl.BlockSpec((1, d, tn), lambda l, j: (l, 0, j)),
                  pl.BlockSpec((1, 1, tn), lambda l, j: (l, 0, j))],
        out_specs=pl.BlockSpec((1, 8, tn), lambda l, j: (l, 0, j)),
        out_shape=jax.ShapeDtypeStruct((depth, 8, d3), F32),
        compiler_params=_params(("parallel", "parallel")),
        name="modulation",
    )(cond, mod_w, mod_b.reshape(depth, 1, d3))


def _norm_mod(x, g, sc, sh):
    ms = jnp.mean(x * x, axis=-1, keepdims=True)
    return (x * lax.rsqrt(ms + EPS)) * (g * (1.0 + sc)) + sh


def _dn_inproj_kernel(x_ref, sh_ref, sc_ref, g_ref, w_ref, ws_ref, ap_ref, o_ref, gates_ref, gates_t_ref, h_scr):
    @pl.when(pl.program_id(2) == 0)
    def _():
        hb = _norm_mod(x_ref[0], g_ref[...], sc_ref[0], sh_ref[0]).astype(BF16)
        h_scr[...] = hb
        raw = jnp.dot(hb, ws_ref[...], preferred_element_type=F32)
        log_decay = -jnp.exp(ap_ref[0:1, :]) * _softplus(raw + ap_ref[1:2, :])
        beta = _sigmoid(raw)
        r = lax.broadcasted_iota(jnp.int32, (2 * CHUNK, CHUNK), 0)
        c = lax.broadcasted_iota(jnp.int32, (2 * CHUNK, CHUNK), 1)
        tri2 = (jnp.where(r < CHUNK, r - c, c - r + CHUNK) >= 0).astype(F32)
        lane = lax.broadcasted_iota(jnp.int32, (CHUNK, LANES), 1)
        for n in range(raw.shape[0] // CHUNK):
            rows = slice(n * CHUNK, (n + 1) * CHUNK)
            cs = _masked_sums(tri2, log_decay[rows])
            gt = jnp.where(lane < DN_HEADS, cs[:CHUNK], jnp.where(lane < 2 * DN_HEADS, cs[CHUNK:], beta[rows]))
            gates_ref[0, rows, :] = gt
            gtt = gt.T
            gates_t_ref[0, :, rows] = jnp.concatenate(
                [jnp.broadcast_to(gtt[m:m + 1], (SUBLANES, CHUNK)) for m in range(2 * DN_HEADS)], axis=0)

    o_ref[0] = jnp.dot(h_scr[...], w_ref[...], preferred_element_type=F32).astype(o_ref.dtype)


def _gla_inproj_kernel(x_ref, sh_ref, sc_ref, g_ref, w_ref, ws_ref, wg_ref, bg_ref, o_ref, lg_ref, h_scr):
    @pl.when(pl.program_id(2) == 0)
    def _():
        hb = _norm_mod(x_ref[0], g_ref[...], sc_ref[0], sh_ref[0]).astype(BF16)
        h_scr[...] = hb
        gl = jnp.dot(hb, ws_ref[...], preferred_element_type=F32)
        zg = jnp.dot(gl.astype(BF16), wg_ref[...], preferred_element_type=F32) + bg_ref[...]
        lg_ref[0] = -_softplus(-zg) * (1.0 / GLA_TAU)

    o_ref[0] = jnp.dot(h_scr[...], w_ref[...], preferred_element_type=F32).astype(o_ref.dtype)


def _inproj(kernel_fn, name, x, sh, sc, g, w_main, w_small, extra, aux_width, aux_transposed=False):
    bsz, t, d = x.shape
    n = w_main.shape[1]
    tm = min(t, 1024)
    tn = 1024
    const = lambda b, i, j: (0, 0)
    extra_specs = [pl.BlockSpec(e.shape, const) for e in extra]
    out_specs = [pl.BlockSpec((1, tm, tn), lambda b, i, j: (b, i, j)),
                 pl.BlockSpec((1, tm, aux_width), lambda b, i, j: (b, i, 0))]
    out_shape = [jax.ShapeDtypeStruct((bsz, t, n), BF16), jax.ShapeDtypeStruct((bsz, t, aux_width), F32)]
    if aux_transposed:
        out_specs.append(pl.BlockSpec((1, aux_width, tm), lambda b, i, j: (b, 0, i)))
        out_shape.append(jax.ShapeDtypeStruct((bsz, aux_width, t), F32))
    return pl.pallas_call(
        kernel_fn,
        grid=(bsz, t // tm, n // tn),
        in_specs=[pl.BlockSpec((1, tm, d), lambda b, i, j: (b, i, 0)),
                  pl.BlockSpec((1, 1, d), lambda b, i, j: (b, 0, 0)),
                  pl.BlockSpec((1, 1, d), lambda b, i, j: (b, 0, 0)),
                  pl.BlockSpec((1, d), const),
                  pl.BlockSpec((d, tn), lambda b, i, j: (0, j)),
                  pl.BlockSpec((d, LANES), const)] + extra_specs,
        out_specs=out_specs,
        out_shape=out_shape,
        scratch_shapes=[pltpu.VMEM((tm, d), BF16)],
        compiler_params=_params(("parallel", "parallel", "arbitrary")),
        name=name,
    )(x, sh, sc, g, w_main, w_small, *extra)


def _conv_kernel(prev_ref, main_ref, next_ref, w_ref, o_ref, *, on_grid, n_q_tiles, n_qk_tiles):
    i = pl.program_id(1)
    j = pl.program_id(2)
    _, tt, tc = main_ref.shape
    strip = GRID_W if on_grid else tt
    n_strips = tt // strip
    pos = lax.broadcasted_iota(jnp.int32, (strip, DN_DK), 0)
    first, last = pos == 0, pos == strip - 1
    not_top = i > 0
    not_bottom = i < pl.num_programs(1) - 1

    def load(r, lanes):
        if r < 0:
            return jnp.where(not_top, prev_ref[0, :, lanes].astype(F32), 0.0)
        if r >= n_strips:
            return jnp.where(not_bottom, next_ref[0, :, lanes].astype(F32), 0.0)
        return main_ref[0, r * strip:(r + 1) * strip, lanes].astype(F32)

    def run(normalise):
        scale = jnp.where(j < n_q_tiles, DN_DK ** -0.5, 1.0)
        for s in range(tc // DN_DK):
            lanes = slice(s * DN_DK, (s + 1) * DN_DK)
            w = [w_ref[k:k + 1, lanes] for k in range(9)]
            if on_grid:
                up, mid = load(-1, lanes), load(0, lanes)
            for r in range(n_strips):
                if on_grid:
                    down = load(r + 1, lanes)
                    accs = [w[dc] * up + w[3 + dc] * mid + w[6 + dc] * down for dc in range(3)]
                    up, mid = mid, down
                else:
                    mid = load(r, lanes)
                    accs = [w[3 + dc] * mid for dc in range(3)]
                y = _silu(accs[1] + jnp.where(first, 0.0, pltpu.roll(accs[0], 1, 0))
                          + jnp.where(last, 0.0, pltpu.roll(accs[2], strip - 1, 0)))
                if normalise:
                    ss = jnp.sum(y * y, axis=-1, keepdims=True)
                    y = y * (lax.rsqrt(ss + EPS) * scale)
                o_ref[0, r * strip:(r + 1) * strip, lanes] = y.astype(o_ref.dtype)

    @pl.when(j < n_qk_tiles)
    def _():
        run(True)

    @pl.when(j >= n_qk_tiles)
    def _():
        run(False)


def _conv(p, conv_w9, on_grid):
    bsz, t, _ = p.shape
    tc = 512
    tt = min(t, 512) if on_grid else t
    rows_per_tile = tt // GRID_W
    n_rows = t // GRID_W
    kern = functools.partial(_conv_kernel, on_grid=on_grid, n_q_tiles=DN_QK // tc, n_qk_tiles=2 * DN_QK // tc)
    return pl.pallas_call(
        kern,
        grid=(bsz, t // tt, DN_QKV // tc),
        in_specs=[pl.BlockSpec((1, GRID_W, tc), lambda b, i, j: (b, jnp.maximum(i * rows_per_tile - 1, 0), j)),
                  pl.BlockSpec((1, tt, tc), lambda b, i, j: (b, i, j)),
                  pl.BlockSpec((1, GRID_W, tc),
                               lambda b, i, j: (b, jnp.minimum((i + 1) * rows_per_tile, n_rows - 1), j)),
                  pl.BlockSpec((9, tc), lambda b, i, j: (0, j))],
        out_specs=pl.BlockSpec((1, tt, tc), lambda b, i, j: (b, i, j)),
        out_shape=jax.ShapeDtypeStruct((bsz, t, DN_QKV), BF16),
        compiler_params=_params(("parallel", "parallel", "parallel")),
        name="conv_grid" if on_grid else "conv_seq",
    )(p, p, p, conv_w9)


_BASE_BLOCK = 8
_M_EYE, _M_DIAG = 4, 5
_MERGE_BLOCKS = tuple(_BASE_BLOCK << n for n in range((CHUNK // _BASE_BLOCK).bit_length() - 1))


def _half_rows(x, b, half):
    return jnp.concatenate([x[s * b:(s + 1) * b] for s in range(half, x.shape[0] // b, 2)], axis=0)


def _put_half_rows(x, upd, b, half):
    return jnp.concatenate([upd[(s // 2) * b:(s // 2 + 1) * b] if s % 2 == half else x[s * b:(s + 1) * b]
                            for s in range(x.shape[0] // b)], axis=0)


def _delta_mask_array():
    r = np.arange(CHUNK)[:, None]
    c = np.arange(CHUNK)[None, :]
    mats = [r >= c, r > c, r <= c, r < c, r == c, (r // _BASE_BLOCK) == (c // _BASE_BLOCK)]
    return jnp.asarray(np.stack(mats).astype(np.float32))


def _delta_merge_mask_array():
    r = np.arange(CHUNK)[:, None]
    c = np.arange(CHUNK)[None, :]
    mats = []
    for b in _MERGE_BLOCKS:
        off = ((r // (2 * b)) == (c // (2 * b))) & ((r // b) != (c // b))
        for rev in (0, 1):
            mats.append(np.concatenate([off[s * b:(s + 1) * b] for s in range(1 - rev, CHUNK // b, 2)], axis=0))
    return jnp.asarray(np.stack(mats).astype(np.float32))


def _gla_mask_array():
    r = np.arange(CHUNK)[:, None]
    c = np.arange(CHUNK)[None, :]
    same = (r // SUB) == (c // SUB)
    mats = [(r >= c) & same, (r // SUB > c // SUB), (r <= c) & same, (r // SUB < c // SUB)]
    return jnp.asarray(np.stack(mats).astype(np.float32))


def _delta_prep_group(qs, ks, vs, gates, gc_rows, revs, heads, m_ref, mm_ref):
    lane = lax.broadcasted_iota(jnp.int32, (CHUNK, LANES), 1)

    def pick(gt, idx):
        return jnp.broadcast_to(jnp.sum(jnp.where(lane == idx, gt, 0.0), axis=1, keepdims=True), (CHUNK, LANES))

    gc = [pick(gt, rev * DN_HEADS + h) for gt, rev, h in zip(gates, revs, heads)]
    beta = [pick(gt, (2 + rev) * DN_HEADS + h) for gt, rev, h in zip(gates, revs, heads)]
    decay = [jnp.exp(jnp.minimum(x - y, 0.0)) for x, y in zip(gc, gc_rows)]
    kq = [_mm_nt(jnp.concatenate([k, q], axis=0), k) for q, k in zip(qs, ks)]
    lm = [x[:CHUNK] * dec * m_ref[2 * rev + 1] * bt for x, dec, rev, bt in zip(kq, decay, revs, beta)]
    a = [(x[CHUNK:] * dec * m_ref[2 * rev]).astype(BF16) for x, dec, rev in zip(kq, decay, revs)]

    lb = [x * m_ref[_M_DIAG] for x in lm]
    l2 = [_mm(x, x) for x in lb]
    l4 = [_mm(x, x) for x in l2]
    inv = [m_ref[_M_EYE] - x for x in lb]
    inv = [x + _mm(x, y) for x, y in zip(inv, l2)]
    inv = [x + _mm(x, y) for x, y in zip(inv, l4)]
    lmb = [x.astype(BF16) for x in lm]
    for n, b in enumerate(_MERGE_BLOCKS):
        own = [_half_rows(x, b, 1 - rev) for x, rev in zip(inv, revs)]
        xc = [jnp.dot(x.astype(BF16), y, preferred_element_type=F32) * mm_ref[2 * n + rev]
              for x, y, rev in zip(own, lmb, revs)]
        own = [x - _mm(y, z) for x, y, z in zip(own, xc, inv)]
        inv = [_put_half_rows(x, y, b, 1 - rev) for x, y, rev in zip(inv, own, revs)]

    kf = [k.astype(F32) for k in ks]
    e_pos = [jnp.exp(x) for x in gc]
    ends = [x[0:1, :] if rev else x[CHUNK - 1:CHUNK, :] for x, rev in zip(gc, revs)]
    rhs = [jnp.concatenate([(k * (bt * e)).astype(BF16),
                            (v.astype(F32) * jnp.concatenate([bt] * (DN_DV // LANES), axis=1)).astype(BF16)], axis=1)
           for k, v, bt, e in zip(kf, vs, beta, e_pos)]
    wu = [jnp.dot(x.astype(BF16), y, preferred_element_type=F32) for x, y in zip(inv, rhs)]
    w = [x[:, :DN_DK].astype(BF16) for x in wu]
    u = [x[:, DN_DK:].astype(BF16) for x in wu]
    kdt = [(k * jnp.exp(end - x)).T.astype(BF16) for k, end, x in zip(kf, ends, gc)]
    aw = [jnp.dot(jnp.concatenate([x, y], axis=0), z, preferred_element_type=F32) for x, y, z in zip(kdt, a, w)]
    lhs = [jnp.concatenate([jnp.concatenate([(-x[:DN_DK]).astype(BF16), y], axis=1),
                            jnp.concatenate([(q.astype(F32) * e - x[DN_DK:]).astype(BF16), z], axis=1)], axis=0)
           for x, y, z, q, e in zip(aw, kdt, a, qs, e_pos)]
    cdec = [jnp.concatenate([jnp.exp(end)] * (DN_DV // LANES), axis=1) for end in ends]
    return lhs, u, cdec


def _delta_scan_kernel(qf_ref, kf_ref, vf_ref, gf_ref, gtf_ref, qb_ref, kb_ref, vb_ref, gb_ref, gtb_ref,
                       s0_ref, m_ref, mm_ref, of_ref, ob_ref, sfin_ref, s_scr, *, nc, par, hpg):
    step = pl.program_id(2)

    @pl.when(step == 0)
    def _():
        s_scr[...] = s0_ref[0]

    fwd_refs = (qf_ref, kf_ref, vf_ref, gf_ref, gtf_ref, of_ref)
    bwd_refs = (qb_ref, kb_ref, vb_ref, gb_ref, gtb_ref, ob_ref)

    def body(it, carry):
        chains = []
        for hh in range(hpg):
            head = pl.program_id(1) * hpg + hh
            for rev in (0, 1):
                for p in range(par):
                    c = nc - 1 - it * par - p if rev else it * par + p
                    chains.append((hh, head, rev, pl.ds(pl.multiple_of(c * CHUNK, CHUNK), CHUNK)))

        def tile(slot, width):
            return [(bwd_refs if rev else fwd_refs)[slot][0, rows, hh * width:(hh + 1) * width]
                    for hh, _, rev, rows in chains]

        gates = [(bwd_refs if rev else fwd_refs)[3][0, rows, :] for _, _, rev, rows in chains]
        gc_rows = [(bwd_refs if rev else fwd_refs)[4][
            0, pl.ds(pl.multiple_of((rev * DN_HEADS + head) * SUBLANES, SUBLANES), SUBLANES), rows][0:1]
            for _, head, rev, rows in chains]
        lhs, u, cdec = _delta_prep_group(tile(0, DN_DK), tile(1, DN_DK), tile(2, DN_DV), gates, gc_rows,
                                         [rev for _, _, rev, _ in chains], [head for _, head, _, _ in chains],
                                         m_ref, mm_ref)
        seqs = [(hh, rev) for hh in range(hpg) for rev in (0, 1)]
        state = [s_scr[hh, rev] for hh, rev in seqs]
        for p in range(par):
            idx = [(hh * 2 + rev) * par + p for hh, rev in seqs]
            res = [jnp.dot(lhs[n], jnp.concatenate([s.astype(BF16), u[n]], axis=0), preferred_element_type=F32)
                   for s, n in zip(state, idx)]
            state = [cdec[n] * s + r[:DN_DK] for s, r, n in zip(state, res, idx)]
            for r, n in zip(res, idx):
                hh, _, rev, rows = chains[n]
                o_ref = (bwd_refs if rev else fwd_refs)[5]
                o_ref[0, rows, hh * DN_DV:(hh + 1) * DN_DV] = r[DN_DK:].astype(o_ref.dtype)
        for (hh, rev), s in zip(seqs, state):
            s_scr[hh, rev] = s
        return carry

    lax.fori_loop(0, nc // par, body, 0)

    @pl.when(step == pl.num_programs(2) - 1)
    def _():
        sfin_ref[0] = s_scr[...]


def _delta_scan(qkv, gates, gates_t, s0, masks, merge_masks):
    bsz, t, _ = qkv.shape
    tb = min(t, 1024)
    ns = t // tb
    hpg = DELTA_HEADS_PER_STEP
    groups = DN_HEADS // hpg

    def fwd(width, off):
        return pl.BlockSpec((1, tb, hpg * width), lambda b, h, s: (b, s, off + h))

    def bwd(width, off):
        return pl.BlockSpec((1, tb, hpg * width), lambda b, h, s: (b, ns - 1 - s, off + h))

    state_spec = pl.BlockSpec((1, hpg, 2, DN_DK, DN_DV), lambda b, h, s: (b, h, 0, 0, 0))
    o_shape = jax.ShapeDtypeStruct((bsz, t, DN_V), BF16)
    return pl.pallas_call(
        functools.partial(_delta_scan_kernel, nc=tb // CHUNK, par=min(SCAN_PAR, tb // CHUNK), hpg=hpg),
        grid=(bsz, groups, ns),
        in_specs=[fwd(DN_DK, 0), fwd(DN_DK, groups), fwd(DN_DV, groups),
                  pl.BlockSpec((1, tb, LANES), lambda b, h, s: (b, s, 0)),
                  pl.BlockSpec((1, LANES, tb), lambda b, h, s: (b, 0, s)),
                  bwd(DN_DK, 0), bwd(DN_DK, groups), bwd(DN_DV, groups),
                  pl.BlockSpec((1, tb, LANES), lambda b, h, s: (b, ns - 1 - s, 0)),
                  pl.BlockSpec((1, LANES, tb), lambda b, h, s: (b, 0, ns - 1 - s)),
                  state_spec,
                  pl.BlockSpec(masks.shape, lambda b, h, s: (0, 0, 0)),
                  pl.BlockSpec(merge_masks.shape, lambda b, h, s: (0, 0, 0))],
        out_specs=[fwd(DN_DV, 0), bwd(DN_DV, 0), state_spec],
        out_shape=[o_shape, o_shape, jax.ShapeDtypeStruct(s0.shape, F32)],
        scratch_shapes=[pltpu.VMEM((hpg, 2, DN_DK, DN_DV), F32)],
        compiler_params=_params(("parallel", "parallel", "arbitrary")),
        name="delta_scan",
    )(qkv, qkv, qkv, gates, gates_t, qkv, qkv, qkv, gates, gates_t, s0, masks, merge_masks)


def _gla_prep_group(qs, ks, lgs, revs, m_ref):
    in_hi = lax.broadcasted_iota(jnp.int32, (CHUNK, GLA_DK), 0) >= SUB
    bc = [_masked_sums(m_ref[2 * rev], lg) for lg, rev in zip(lgs, revs)]

    def per_block(x, rows):
        return jnp.concatenate([jnp.broadcast_to(x[r:r + 1], (SUB, GLA_DK)) for r in rows], axis=0)

    ref_rows = [(SUB // 2 - 1, SUB + SUB // 2 - 1) if rev else (SUB // 2, SUB + SUB // 2) for rev in revs]
    last_rows = [(0, SUB) if rev else (SUB - 1, CHUNK - 1) for rev in revs]
    refb = [per_block(x, rows) for x, rows in zip(bc, ref_rows)]
    lastb = [per_block(x, rows) for x, rows in zip(bc, last_rows)]
    qf = [q.astype(F32) * (GLA_DK ** -0.5) for q in qs]
    kf = [k.astype(F32) for k in ks]
    qe = [q * jnp.exp(x - r) for q, x, r in zip(qf, bc, refb)]
    ke = [k * jnp.exp(r - x) for k, x, r in zip(kf, bc, refb)]
    qd = [q * jnp.exp(x) for q, x in zip(qf, bc)]
    kd = [k * jnp.exp(l - x) for k, x, l in zip(kf, bc, lastb)]
    a_in = [_mm_nt(x, y) for x, y in zip(qe, ke)]
    a_x = [_mm_nt(x, y) for x, y in zip(qd, kd)]
    a = [x * m_ref[2 * rev] + y * m_ref[2 * rev + 1] for x, y, rev in zip(a_in, a_x, revs)]
    last1 = [x[rows[1]:rows[1] + 1] if rev else x[rows[0]:rows[0] + 1] for x, rows, rev in zip(bc, last_rows, revs)]
    last2 = [x[rows[0]:rows[0] + 1] if rev else x[rows[1]:rows[1] + 1] for x, rows, rev in zip(bc, last_rows, revs)]
    second = [jnp.logical_not(in_hi) if rev else in_hi for rev in revs]
    qd2 = [x * jnp.where(sec, jnp.exp(l1), 1.0) for x, sec, l1 in zip(qd, second, last1)]
    kd2 = [x * jnp.where(sec, 1.0, jnp.exp(l2)) for x, sec, l2 in zip(kd, second, last2)]
    cdcol = [jnp.broadcast_to(jnp.exp(l1 + l2), (GLA_DK, LANES)).T for l1, l2 in zip(last1, last2)]
    lhs = [jnp.concatenate([x.astype(BF16), y.astype(BF16)], axis=1) for x, y in zip(qd2, a)]
    kdt = [x.T.astype(BF16) for x in kd2]
    decay = [jnp.concatenate([x] * (GLA_DV // LANES), axis=1) for x in cdcol]
    return lhs, kdt, decay


def _gla_scan_kernel(qf_ref, kf_ref, vf_ref, gf_ref, qb_ref, kb_ref, vb_ref, gb_ref, s0_ref, m_ref,
                     of_ref, ob_ref, sfin_ref, s_scr, *, nc, par):
    step = pl.program_id(2)

    @pl.when(step == 0)
    def _():
        s_scr[...] = s0_ref[0, 0]

    def body(it, carry):
        rows = ([pl.ds(pl.multiple_of((it * par + p) * CHUNK, CHUNK), CHUNK) for p in range(par)]
                + [pl.ds(pl.multiple_of((nc - 1 - it * par - p) * CHUNK, CHUNK), CHUNK) for p in range(par)])
        refs = [(qf_ref, kf_ref, vf_ref, gf_ref, of_ref)] * par + [(qb_ref, kb_ref, vb_ref, gb_ref, ob_ref)] * par
        revs = [0] * par + [1] * par
        lhs, kdt, decay = _gla_prep_group([r[0][0, rw, :] for r, rw in zip(refs, rows)],
                                          [r[1][0, rw, :] for r, rw in zip(refs, rows)],
                                          [r[3][0, rw, :] for r, rw in zip(refs, rows)], revs, m_ref)
        vs = [r[2][0, rw, :] for r, rw in zip(refs, rows)]
        upd = [jnp.dot(x, v, preferred_element_type=F32) for x, v in zip(kdt, vs)]
        state = [s_scr[0], s_scr[1]]
        for p in range(par):
            idx = [p, par + p]
            outs = [jnp.dot(lhs[n], jnp.concatenate([state[d].astype(BF16), vs[n]], axis=0),
                            preferred_element_type=F32) for d, n in enumerate(idx)]
            state = [decay[n] * state[d] + upd[n] for d, n in enumerate(idx)]
            for d, n in enumerate(idx):
                refs[n][4][0, rows[n], :] = outs[d].astype(refs[n][4].dtype)
        s_scr[0] = state[0]
        s_scr[1] = state[1]
        return carry

    lax.fori_loop(0, nc // par, body, 0)

    @pl.when(step == pl.num_programs(2) - 1)
    def _():
        sfin_ref[0, 0] = s_scr[...]


def _gla_scan(p, lg, s0, masks):
    bsz, t, _ = p.shape
    tb = min(t, 1024)
    ns = t // tb
    q_off = GLA_V // GLA_DK
    k_off = q_off + GLA_HEADS
    v_off = (GLA_V + 2 * GLA_K) // GLA_DV

    def spec(width, off, rev):
        if rev:
            return pl.BlockSpec((1, tb, width), lambda b, h, s: (b, ns - 1 - s, off + h))
        return pl.BlockSpec((1, tb, width), lambda b, h, s: (b, s, off + h))

    state_spec = pl.BlockSpec((1, 1, 2, GLA_DK, GLA_DV), lambda b, h, s: (b, h, 0, 0, 0))
    o_shape = jax.ShapeDtypeStruct((bsz, t, GLA_V), BF16)
    return pl.pallas_call(
        functools.partial(_gla_scan_kernel, nc=tb // CHUNK, par=min(SCAN_PAR, tb // CHUNK)),
        grid=(bsz, GLA_HEADS, ns),
        in_specs=[spec(GLA_DK, q_off, 0), spec(GLA_DK, k_off, 0), spec(GLA_DV, v_off, 0), spec(GLA_DK, 0, 0),
                  spec(GLA_DK, q_off, 1), spec(GLA_DK, k_off, 1), spec(GLA_DV, v_off, 1),
                  spec(GLA_DK, GLA_HEADS, 1),
                  state_spec,
                  pl.BlockSpec(masks.shape, lambda b, h, s: (0, 0, 0))],
        out_specs=[spec(GLA_DV, 0, 0), spec(GLA_DV, 0, 1), state_spec],
        out_shape=[o_shape, o_shape, jax.ShapeDtypeStruct(s0.shape, F32)],
        scratch_shapes=[pltpu.VMEM((2, GLA_DK, GLA_DV), F32)],
        compiler_params=_params(("parallel", "parallel", "arbitrary")),
        name="gla_scan",
    )(p, p, p, lg, p, p, p, lg, s0, masks)


def _gated_out_kernel(of_ref, ob_ref, z_ref, x_ref, gt_ref, ng_ref, w_ref, fg_ref, o_ref, *, dv, final):
    o = of_ref[0].astype(F32) + ob_ref[0].astype(F32)
    parts = []
    for s in range(o.shape[1] // dv):
        seg = o[:, s * dv:(s + 1) * dv]
        ms = jnp.mean(seg * seg, axis=-1, keepdims=True)
        parts.append(seg * lax.rsqrt(ms + EPS))
    y = jnp.concatenate(parts, axis=1) * ng_ref[...] * _silu(z_ref[0].astype(F32))
    proj = jnp.dot(y.astype(BF16), w_ref[...], preferred_element_type=F32)
    xn = x_ref[0] + gt_ref[0] * proj
    if final:
        ms = jnp.mean(xn * xn, axis=-1, keepdims=True)
        xn = xn * lax.rsqrt(ms + EPS) * fg_ref[...]
    o_ref[0] = xn


def _gated_out(o_f, o_b, p, z_block, x, gt, norm_g_row, w_out, final_g_row, dv, final):
    bsz, t, d = x.shape
    dvt = o_f.shape[2]
    tm = min(t, 512)
    const = lambda b, i: (0, 0)
    return pl.pallas_call(
        functools.partial(_gated_out_kernel, dv=dv, final=final),
        grid=(bsz, t // tm),
        in_specs=[pl.BlockSpec((1, tm, dvt), lambda b, i: (b, i, 0)),
                  pl.BlockSpec((1, tm, dvt), lambda b, i: (b, i, 0)),
                  pl.BlockSpec((1, tm, dvt), lambda b, i: (b, i, z_block)),
                  pl.BlockSpec((1, tm, d), lambda b, i: (b, i, 0)),
                  pl.BlockSpec((1, 1, d), lambda b, i: (b, 0, 0)),
                  pl.BlockSpec((1, dvt), const),
                  pl.BlockSpec((dvt, d), const),
                  pl.BlockSpec((1, d), const)],
        out_specs=pl.BlockSpec((1, tm, d), lambda b, i: (b, i, 0)),
        out_shape=jax.ShapeDtypeStruct((bsz, t, d), F32),
        compiler_params=_params(("parallel", "parallel")),
        name="gated_out_final" if final else "gated_out",
    )(o_f, o_b, p, x, gt, norm_g_row, w_out, final_g_row)


def _pad_cols(w, width):
    return jnp.pad(w, ((0, 0), (0, width - w.shape[1])))


def kernel(x, c, ctx, c_ctx, mod_w, mod_b, norm_g, dn_w_in, dn_conv_w, dn_a_log, dn_dt_bias, dn_norm_g,
           dn_w_out, gla_w_in, gla_w_g2, gla_b_g, gla_norm_g, gla_w_out, final_g):
    bsz, _, d = x.shape
    assert bsz + 1 <= 8
    cond = jnp.concatenate([c, c_ctx[None], jnp.zeros((8 - bsz - 1, d), F32)], axis=0)
    mod = _modulation(cond, mod_w, mod_b)

    def mod_rows(layer):
        lat = [mod[layer, :bsz, None, n * d:(n + 1) * d] for n in range(3)]
        cx = [jnp.broadcast_to(mod[layer, bsz, n * d:(n + 1) * d], (bsz, 1, d)) for n in range(3)]
        return lat, cx

    final_row = final_g.reshape(1, d)

    (sh_l, sc_l, gt_l), (sh_c, sc_c, gt_c) = mod_rows(0)
    w0 = dn_w_in[0].astype(BF16)
    w0_main, w0_small = w0[:, :DN_MAIN], _pad_cols(w0[:, DN_MAIN:], LANES)
    gate_params = jnp.zeros((8, LANES), F32)
    gate_params = gate_params.at[0, :2 * DN_HEADS].set(dn_a_log[0].reshape(-1))
    gate_params = gate_params.at[1, :2 * DN_HEADS].set(dn_dt_bias[0].reshape(-1))
    g0 = norm_g[0].reshape(1, d)
    conv_w9 = dn_conv_w[0].reshape(9, DN_QKV)
    dmasks = (_delta_mask_array(), _delta_merge_mask_array())
    dn_g_row = jnp.tile(dn_norm_g[0], DN_HEADS).reshape(1, DN_V)
    dn_wo = dn_w_out[0].astype(BF16)

    def dn_features(u, sh, sc, on_grid):
        p, gates, gates_t = _inproj(_dn_inproj_kernel, "dn_inproj", u, sh, sc, g0, w0_main, w0_small,
                                    [gate_params], LANES, aux_transposed=True)
        return p, gates, gates_t, _conv(p, conv_w9, on_grid)

    p_c, gates_c, gates_tc, qkv_c = dn_features(ctx, sh_c, sc_c, False)
    p_l, gates_l, gates_tl, qkv_l = dn_features(x, sh_l, sc_l, True)
    s_zero = jnp.zeros((bsz, DN_HEADS, 2, DN_DK, DN_DV), F32)
    oc_f, oc_b, s_ctx = _delta_scan(qkv_c, gates_c, gates_tc, s_zero, *dmasks)
    ol_f, ol_b, _ = _delta_scan(qkv_l, gates_l, gates_tl, s_ctx, *dmasks)
    z_block = DN_QKV // DN_V
    ctx1 = _gated_out(oc_f, oc_b, p_c, z_block, ctx, gt_c, dn_g_row, dn_wo, final_row, DN_DV, False)
    x1 = _gated_out(ol_f, ol_b, p_l, z_block, x, gt_l, dn_g_row, dn_wo, final_row, DN_DV, False)

    (sh_l, sc_l, gt_l), (sh_c, sc_c, _) = mod_rows(1)
    w1 = gla_w_in[0].astype(BF16)
    w1_main = jnp.concatenate([w1[:, 2 * GLA_K + GLA_V:GLA_MAIN], w1[:, :2 * GLA_K + GLA_V]], axis=1)
    w1_small = _pad_cols(w1[:, GLA_MAIN:], LANES)
    wg = jnp.zeros((LANES, 2 * GLA_K), F32)
    wg = wg.at[:GLA_RANK, :GLA_K].set(gla_w_g2[0, 0]).at[GLA_RANK:2 * GLA_RANK, GLA_K:].set(gla_w_g2[0, 1])
    wg = wg.astype(BF16)
    bg = gla_b_g[0].reshape(1, 2 * GLA_K)
    g1 = norm_g[1].reshape(1, d)
    gmasks = _gla_mask_array()
    gla_g_row = jnp.tile(gla_norm_g[0], GLA_HEADS).reshape(1, GLA_V)
    gla_wo = gla_w_out[0].astype(BF16)

    def gla_features(u, sh, sc):
        return _inproj(_gla_inproj_kernel, "gla_inproj", u, sh, sc, g1, w1_main, w1_small, [wg, bg], 2 * GLA_K)

    pg_c, lg_c = gla_features(ctx1, sh_c, sc_c)
    pg_l, lg_l = gla_features(x1, sh_l, sc_l)
    s_zero = jnp.zeros((bsz, GLA_HEADS, 2, GLA_DK, GLA_DV), F32)
    _, _, s_ctx = _gla_scan(pg_c, lg_c, s_zero, gmasks)
    og_f, og_b, _ = _gla_scan(pg_l, lg_l, s_ctx, gmasks)
    return _gated_out(og_f, og_b, pg_l, 0, x1, gt_l, gla_g_row, gla_wo, final_row, GLA_DV, True)
```

```python
import functools

import numpy as np
import jax
import jax.numpy as jnp
from jax import lax
from jax.experimental import pallas as pl
from jax.experimental.pallas import tpu as pltpu

F32, BF16 = jnp.float32, jnp.bfloat16
HIGHEST = lax.Precision.HIGHEST
EPS = 1e-6

LANES = 128
SUBLANES = 8
GRID_W = 64
CHUNK = 128
SUB = 64
SCAN_PAR = 4
GLA_SCAN_PAR = 4
DELTA_HEADS_PER_STEP = 2
VMEM_LIMIT_BYTES = 52 * 1024 * 1024
FUSED_VMEM_LIMIT_BYTES = 58 * 1024 * 1024

DN_DK, DN_HEADS, DN_DV = 128, 8, 256
CONV_K = 3
DN_QK = DN_HEADS * DN_DK
DN_V = DN_HEADS * DN_DV
DN_QKV = 2 * DN_QK + DN_V
DN_MAIN = DN_QKV + DN_V
GLA_HEADS, GLA_DK, GLA_DV, GLA_RANK, GLA_TAU = 4, 128, 512, 16, 16.0
GLA_K = GLA_HEADS * GLA_DK
GLA_V = GLA_HEADS * GLA_DV
GLA_MAIN = 2 * GLA_K + 2 * GLA_V


def _sigmoid(x):
    return 1.0 / (1.0 + jnp.exp(-x))


def _silu(x):
    return x * _sigmoid(x)


def _softplus(x):
    return jnp.maximum(x, 0.0) + jnp.log(1.0 + jnp.exp(-jnp.abs(x)))


def _mm(a, b):
    return jnp.dot(a.astype(BF16), b.astype(BF16), preferred_element_type=F32)


def _mm_nt(a, b):
    return lax.dot_general(a.astype(BF16), b.astype(BF16), (((1,), (1,)), ((), ())),
                           preferred_element_type=F32)


def _masked_sums(mask01, x):
    hi = x.astype(BF16)
    r1 = x - hi.astype(F32)
    mid = r1.astype(BF16)
    lo = (r1 - mid.astype(F32)).astype(BF16)
    n = x.shape[1]
    s = jnp.dot(mask01.astype(BF16), jnp.concatenate([hi, mid, lo], axis=1), preferred_element_type=F32)
    return (s[:, :n] + s[:, n:2 * n]) + s[:, 2 * n:]


def _mm_f32(a, b):
    return jnp.dot(a, b, precision=HIGHEST, preferred_element_type=F32)


def _params(semantics):
    return pltpu.CompilerParams(dimension_semantics=semantics, vmem_limit_bytes=VMEM_LIMIT_BYTES)


def _mod_kernel(c_ref, w_ref, b_ref, o_ref):
    o_ref[0] = _mm_f32(_silu(c_ref[...]), w_ref[0]) + b_ref[0]


def _modulation(cond, mod_w, mod_b):
    depth, d, d3 = mod_w.shape
    tn = d
    return pl.pallas_call(
        _mod_kernel,
        grid=(depth, d3 // tn),
        in_specs=[pl.BlockSpec((8, d), lambda l, j: (0, 0)),
                  pl.BlockSpec((1, d, tn), lambda l, j: (l, 0, j)),
                  pl.BlockSpec((1, 1, tn), lambda l, j: (l, 0, j))],
        out_specs=pl.BlockSpec((1, 8, tn), lambda l, j: (l, 0, j)),
        out_shape=jax.ShapeDtypeStruct((depth, 8, d3), F32),
        compiler_params=_params(("parallel", "parallel")),
        name="modulation",
    )(cond, mod_w, mod_b.reshape(depth, 1, d3))


NORM_STRIP = 32


def _norm_mod_to(h_scr, x_ref, g_ref, sc_ref, sh_ref, dst_row=0, valid=None):
    gain = g_ref[...] * (1.0 + sc_ref[0])
    shift = sh_ref[0]
    n_strips = x_ref.shape[1] // NORM_STRIP

    def strip(n, carry):
        x = x_ref[0, pl.ds(pl.multiple_of(n * NORM_STRIP, NORM_STRIP), NORM_STRIP), :]
        ms = jnp.mean(x * x, axis=-1, keepdims=True)
        h = (x * lax.rsqrt(ms + EPS)) * gain + shift
        if valid is not None:
            h = jnp.where(valid, h, 0.0)
        h_scr[pl.ds(pl.multiple_of(dst_row + n * NORM_STRIP, NORM_STRIP), NORM_STRIP), :] = h.astype(BF16)
        return carry

    lax.fori_loop(0, n_strips, strip, 0, unroll=min(8, n_strips))


def _dn_gates(hb, ws_ref, ap_ref, gates_ref, gates_t_ref):
    raw = jnp.dot(hb, ws_ref[...], preferred_element_type=F32)
    log_decay = -jnp.exp(ap_ref[0:1, :]) * _softplus(raw + ap_ref[1:2, :])
    beta = _sigmoid(raw)
    r = lax.broadcasted_iota(jnp.int32, (2 * CHUNK, CHUNK), 0)
    c = lax.broadcasted_iota(jnp.int32, (2 * CHUNK, CHUNK), 1)
    tri2 = (jnp.where(r < CHUNK, r - c, c - r + CHUNK) >= 0).astype(F32)
    lane = lax.broadcasted_iota(jnp.int32, (CHUNK, LANES), 1)
    for n in range(raw.shape[0] // CHUNK):
        rows = slice(n * CHUNK, (n + 1) * CHUNK)
        cs = _masked_sums(tri2, log_decay[rows])
        gt = jnp.where(lane < DN_HEADS, cs[:CHUNK], jnp.where(lane < 2 * DN_HEADS, cs[CHUNK:], beta[rows]))
        gates_ref[0, rows, :] = gt
        gtt = gt.T
        gates_t_ref[0, :, rows] = jnp.concatenate(
            [jnp.broadcast_to(gtt[m:m + 1], (SUBLANES, CHUNK)) for m in range(2 * DN_HEADS)], axis=0)


def _dn_inproj_kernel(x_ref, sh_ref, sc_ref, g_ref, w_ref, ws_ref, ap_ref, o_ref, gates_ref, gates_t_ref, h_scr):
    @pl.when(pl.program_id(2) == 0)
    def _():
        _norm_mod_to(h_scr, x_ref, g_ref, sc_ref, sh_ref)
        _dn_gates(h_scr[...], ws_ref, ap_ref, gates_ref, gates_t_ref)

    o_ref[0] = jnp.dot(h_scr[...], w_ref[...], preferred_element_type=F32).astype(o_ref.dtype)


def _conv_strip(up, mid, down, w, first, last, scale):
    accs = [w[dc] * up + w[3 + dc] * mid + w[6 + dc] * down for dc in range(3)]
    y = _silu(accs[1] + jnp.where(first, 0.0, pltpu.roll(accs[0], 1, 0))
              + jnp.where(last, 0.0, pltpu.roll(accs[2], GRID_W - 1, 0)))
    if scale is not None:
        y = y * (lax.rsqrt(jnp.sum(y * y, axis=-1, keepdims=True) + EPS) * scale)
    return y


def _dn_inproj_conv_kernel(xp_ref, x_ref, xn_ref, sh_ref, sc_ref, g_ref, w_ref, ws_ref, ap_ref, cw_ref,
                           qkv_ref, z_ref, gates_ref, gates_t_ref, h_scr, p_scr, *, tw):
    i = pl.program_id(1)
    tm = x_ref.shape[1]
    n_rows = tm // GRID_W
    _norm_mod_to(h_scr, xp_ref, g_ref, sc_ref, sh_ref, 0, i > 0)
    _norm_mod_to(h_scr, x_ref, g_ref, sc_ref, sh_ref, GRID_W)
    _norm_mod_to(h_scr, xn_ref, g_ref, sc_ref, sh_ref, GRID_W + tm, i < pl.num_programs(1) - 1)
    _dn_gates(h_scr[GRID_W:GRID_W + tm, :], ws_ref, ap_ref, gates_ref, gates_t_ref)

    pos = lax.broadcasted_iota(jnp.int32, (GRID_W, DN_DK), 0)
    first, last = pos == 0, pos == GRID_W - 1
    n_conv = DN_QKV // tw
    n_z = DN_V // tw

    def conv_tile(t):
        cols0 = t * tw
        scale = DN_DK ** -0.5 if cols0 < DN_QK else (1.0 if cols0 < 2 * DN_QK else None)
        for s in range(tw // DN_DK):
            lanes = slice(s * DN_DK, (s + 1) * DN_DK)
            w = [cw_ref[k:k + 1, cols0 + s * DN_DK:cols0 + (s + 1) * DN_DK] for k in range(9)]
            up = p_scr[t % 2, 0:GRID_W, lanes]
            mid = p_scr[t % 2, GRID_W:2 * GRID_W, lanes]
            for r in range(n_rows):
                down = p_scr[t % 2, (r + 2) * GRID_W:(r + 3) * GRID_W, lanes]
                y = _conv_strip(up, mid, down, w, first, last, scale)
                qkv_ref[0, r * GRID_W:(r + 1) * GRID_W, cols0 + s * DN_DK:cols0 + (s + 1) * DN_DK] = (
                    y.astype(qkv_ref.dtype))
                up, mid = mid, down

    for t in range(n_conv):
        p_scr[t % 2] = jnp.dot(h_scr[...], w_ref[:, t * tw:(t + 1) * tw], preferred_element_type=F32)
        if t > 0:
            conv_tile(t - 1)
    for t in range(n_z):
        z_ref[0, :, t * tw:(t + 1) * tw] = jnp.dot(
            h_scr[GRID_W:GRID_W + tm, :], w_ref[:, DN_QKV + t * tw:DN_QKV + (t + 1) * tw],
            preferred_element_type=F32).astype(z_ref.dtype)
        if t == 0:
            conv_tile(n_conv - 1)


def _dn_inproj_conv(x, sh, sc, g, w_main, w_small, gate_params, conv_w9):
    bsz, t, d = x.shape
    tm = 512
    tw = 1024
    rows_per_tile = tm // GRID_W
    n_rows = t // GRID_W
    const = lambda b, i: (0, 0)
    return pl.pallas_call(
        functools.partial(_dn_inproj_conv_kernel, tw=tw),
        grid=(bsz, t // tm),
        in_specs=[pl.BlockSpec((1, GRID_W, d), lambda b, i: (b, jnp.maximum(i * rows_per_tile - 1, 0), 0)),
                  pl.BlockSpec((1, tm, d), lambda b, i: (b, i, 0)),
                  pl.BlockSpec((1, GRID_W, d), lambda b, i: (b, jnp.minimum((i + 1) * rows_per_tile, n_rows - 1), 0)),
                  pl.BlockSpec((1, 1, d), lambda b, i: (b, 0, 0)),
                  pl.BlockSpec((1, 1, d), lambda b, i: (b, 0, 0)),
                  pl.BlockSpec((1, d), const),
                  pl.BlockSpec(w_main.shape, const),
                  pl.BlockSpec((d, LANES), const),
                  pl.BlockSpec(gate_params.shape, const),
                  pl.BlockSpec(conv_w9.shape, const)],
        out_specs=[pl.BlockSpec((1, tm, DN_QKV), lambda b, i: (b, i, 0)),
                   pl.BlockSpec((1, tm, DN_V), lambda b, i: (b, i, 0)),
                   pl.BlockSpec((1, tm, LANES), lambda b, i: (b, i, 0)),
                   pl.BlockSpec((1, LANES, tm), lambda b, i: (b, 0, i))],
        out_shape=[jax.ShapeDtypeStruct((bsz, t, DN_QKV), BF16), jax.ShapeDtypeStruct((bsz, t, DN_V), BF16),
                   jax.ShapeDtypeStruct((bsz, t, LANES), F32), jax.ShapeDtypeStruct((bsz, LANES, t), F32)],
        scratch_shapes=[pltpu.VMEM((tm + 2 * GRID_W, d), BF16), pltpu.VMEM((2, tm + 2 * GRID_W, tw), F32)],
        compiler_params=pltpu.CompilerParams(dimension_semantics=("parallel", "parallel"),
                                             vmem_limit_bytes=FUSED_VMEM_LIMIT_BYTES),
        name="dn_inproj_conv",
    )(x, x, x, sh, sc, g, w_main, w_small, gate_params, conv_w9)


def _gla_inproj_kernel(x_ref, sh_ref, sc_ref, g_ref, w_ref, ws_ref, wg_ref, bg_ref, o_ref, lg_ref, h_scr):
    @pl.when(pl.program_id(2) == 0)
    def _():
        _norm_mod_to(h_scr, x_ref, g_ref, sc_ref, sh_ref)
        gl = jnp.dot(h_scr[...], ws_ref[...], preferred_element_type=F32)
        glb = gl.astype(BF16)
        for r0 in range(0, glb.shape[0], CHUNK):
            zg = jnp.dot(glb[r0:r0 + CHUNK], wg_ref[...], preferred_element_type=F32) + bg_ref[...]
            lg_ref[0, r0:r0 + CHUNK, :] = -_softplus(-zg) * (1.0 / GLA_TAU)

    o_ref[0] = jnp.dot(h_scr[...], w_ref[...], preferred_element_type=F32).astype(o_ref.dtype)


def _inproj(kernel_fn, name, x, sh, sc, g, w_main, w_small, extra, aux_width, aux_transposed=False):
    bsz, t, d = x.shape
    n = w_main.shape[1]
    tm = min(t, 1024)
    tn = 1024
    const = lambda b, i, j: (0, 0)
    extra_specs = [pl.BlockSpec(e.shape, const) for e in extra]
    out_specs = [pl.BlockSpec((1, tm, tn), lambda b, i, j: (b, i, j)),
                 pl.BlockSpec((1, tm, aux_width), lambda b, i, j: (b, i, 0))]
    out_shape = [jax.ShapeDtypeStruct((bsz, t, n), BF16), jax.ShapeDtypeStruct((bsz, t, aux_width), F32)]
    if aux_transposed:
        out_specs.append(pl.BlockSpec((1, aux_width, tm), lambda b, i, j: (b, 0, i)))
        out_shape.append(jax.ShapeDtypeStruct((bsz, aux_width, t), F32))
    return pl.pallas_call(
        kernel_fn,
        grid=(bsz, t // tm, n // tn),
        in_specs=[pl.BlockSpec((1, tm, d), lambda b, i, j: (b, i, 0)),
                  pl.BlockSpec((1, 1, d), lambda b, i, j: (b, 0, 0)),
                  pl.BlockSpec((1, 1, d), lambda b, i, j: (b, 0, 0)),
                  pl.BlockSpec((1, d), const),
                  pl.BlockSpec((d, tn), lambda b, i, j: (0, j)),
                  pl.BlockSpec((d, LANES), const)] + extra_specs,
        out_specs=out_specs,
        out_shape=out_shape,
        scratch_shapes=[pltpu.VMEM((tm, d), BF16)],
        compiler_params=_params(("parallel", "parallel", "arbitrary")),
        name=name,
    )(x, sh, sc, g, w_main, w_small, *extra)


def _conv_seq_kernel(p_ref, w_ref, o_ref, *, n_q_tiles, n_qk_tiles):
    j = pl.program_id(1)
    _, t, tc = p_ref.shape
    pos = lax.broadcasted_iota(jnp.int32, (t, DN_DK), 0)
    first, last = pos == 0, pos == t - 1

    def run(normalise):
        scale = jnp.where(j < n_q_tiles, DN_DK ** -0.5, 1.0)
        for s in range(tc // DN_DK):
            lanes = slice(s * DN_DK, (s + 1) * DN_DK)
            x = p_ref[0, :, lanes].astype(F32)
            w = [w_ref[CONV_K + dc:CONV_K + dc + 1, lanes] for dc in range(CONV_K)]
            y = _silu(w[1] * x + jnp.where(first, 0.0, pltpu.roll(w[0] * x, 1, 0))
                      + jnp.where(last, 0.0, pltpu.roll(w[2] * x, t - 1, 0)))
            if normalise:
                y = y * (lax.rsqrt(jnp.sum(y * y, axis=-1, keepdims=True) + EPS) * scale)
            o_ref[0, :, lanes] = y.astype(o_ref.dtype)

    @pl.when(j < n_qk_tiles)
    def _():
        run(True)

    @pl.when(j >= n_qk_tiles)
    def _():
        run(False)


def _conv_seq(p, conv_w9):
    bsz, t, _ = p.shape
    tc = 512
    kern = functools.partial(_conv_seq_kernel, n_q_tiles=DN_QK // tc, n_qk_tiles=2 * DN_QK // tc)
    return pl.pallas_call(
        kern,
        grid=(bsz, DN_QKV // tc),
        in_specs=[pl.BlockSpec((1, t, tc), lambda b, j: (b, 0, j)),
                  pl.BlockSpec((CONV_K * CONV_K, tc), lambda b, j: (0, j))],
        out_specs=pl.BlockSpec((1, t, tc), lambda b, j: (b, 0, j)),
        out_shape=jax.ShapeDtypeStruct((bsz, t, DN_QKV), BF16),
        compiler_params=_params(("parallel", "parallel")),
        name="conv_seq",
    )(p, conv_w9)


_BASE_BLOCK = 8
_M_EYE, _M_DIAG = 4, 5
_MERGE_BLOCKS = tuple(_BASE_BLOCK << n for n in range((CHUNK // _BASE_BLOCK).bit_length() - 1))


def _half_rows(x, b, half):
    return jnp.concatenate([x[s * b:(s + 1) * b] for s in range(half, x.shape[0] // b, 2)], axis=0)


def _put_half_rows(x, upd, b, half):
    return jnp.concatenate([upd[(s // 2) * b:(s // 2 + 1) * b] if s % 2 == half else x[s * b:(s + 1) * b]
                            for s in range(x.shape[0] // b)], axis=0)


def _delta_mask_array():
    r = np.arange(CHUNK)[:, None]
    c = np.arange(CHUNK)[None, :]
    mats = [r >= c, r > c, r <= c, r < c, r == c, (r // _BASE_BLOCK) == (c // _BASE_BLOCK)]
    return jnp.asarray(np.stack(mats).astype(np.float32))


def _delta_merge_mask_array():
    r = np.arange(CHUNK)[:, None]
    c = np.arange(CHUNK)[None, :]
    mats = []
    for b in _MERGE_BLOCKS:
        off = ((r // (2 * b)) == (c // (2 * b))) & ((r // b) != (c // b))
        for rev in (0, 1):
            mats.append(np.concatenate([off[s * b:(s + 1) * b] for s in range(1 - rev, CHUNK // b, 2)], axis=0))
    return jnp.asarray(np.stack(mats).astype(np.float32))


def _gla_mask_array():
    r = np.arange(CHUNK)[:, None]
    c = np.arange(CHUNK)[None, :]
    same = (r // SUB) == (c // SUB)
    mats = [(r >= c) & same, (r // SUB > c // SUB), (r <= c) & same, (r // SUB < c // SUB)]
    return jnp.asarray(np.stack(mats).astype(np.float32))


def _delta_prep_group(qs, ks, vs, gates, gc_rows, revs, heads, m_ref, mm_ref):
    lane = lax.broadcasted_iota(jnp.int32, (CHUNK, LANES), 1)

    def pick(gt, idx):
        return jnp.broadcast_to(jnp.sum(jnp.where(lane == idx, gt, 0.0), axis=1, keepdims=True), (CHUNK, LANES))

    gc = [pick(gt, rev * DN_HEADS + h) for gt, rev, h in zip(gates, revs, heads)]
    beta = [pick(gt, (2 + rev) * DN_HEADS + h) for gt, rev, h in zip(gates, revs, heads)]
    decay = [jnp.exp(jnp.minimum(x - y, 0.0)) for x, y in zip(gc, gc_rows)]
    kq = [_mm_nt(jnp.concatenate([k, q], axis=0), k) for q, k in zip(qs, ks)]
    lm = [x[:CHUNK] * dec * m_ref[2 * rev + 1] * bt for x, dec, rev, bt in zip(kq, decay, revs, beta)]
    a = [(x[CHUNK:] * dec * m_ref[2 * rev]).astype(BF16) for x, dec, rev in zip(kq, decay, revs)]

    lb = [x * m_ref[_M_DIAG] for x in lm]
    l2 = [_mm(x, x) for x in lb]
    l4 = [_mm(x, x) for x in l2]
    inv = [m_ref[_M_EYE] - x for x in lb]
    inv = [x + _mm(x, y) for x, y in zip(inv, l2)]
    inv = [x + _mm(x, y) for x, y in zip(inv, l4)]
    lmb = [x.astype(BF16) for x in lm]
    for n, b in enumerate(_MERGE_BLOCKS):
        own = [_half_rows(x, b, 1 - rev) for x, rev in zip(inv, revs)]
        xc = [jnp.dot(x.astype(BF16), y, preferred_element_type=F32) * mm_ref[2 * n + rev]
              for x, y, rev in zip(own, lmb, revs)]
        own = [x - _mm(y, z) for x, y, z in zip(own, xc, inv)]
        inv = [_put_half_rows(x, y, b, 1 - rev) for x, y, rev in zip(inv, own, revs)]

    kf = [k.astype(F32) for k in ks]
    e_pos = [jnp.exp(x) for x in gc]
    ends = [x[0:1, :] if rev else x[CHUNK - 1:CHUNK, :] for x, rev in zip(gc, revs)]
    rhs = [jnp.concatenate([(k * (bt * e)).astype(BF16),
                            (v.astype(F32) * jnp.concatenate([bt] * (DN_DV // LANES), axis=1)).astype(BF16)], axis=1)
           for k, v, bt, e in zip(kf, vs, beta, e_pos)]
    wu = [jnp.dot(x.astype(BF16), y, preferred_element_type=F32) for x, y in zip(inv, rhs)]
    w = [x[:, :DN_DK].astype(BF16) for x in wu]
    u = [x[:, DN_DK:].astype(BF16) for x in wu]
    kdt = [(k * jnp.exp(end - x)).T.astype(BF16) for k, end, x in zip(kf, ends, gc)]
    aw = [jnp.dot(jnp.concatenate([x, y], axis=0), z, preferred_element_type=F32) for x, y, z in zip(kdt, a, w)]
    lhs = [jnp.concatenate([jnp.concatenate([(-x[:DN_DK]).astype(BF16), y], axis=1),
                            jnp.concatenate([(q.astype(F32) * e - x[DN_DK:]).astype(BF16), z], axis=1)], axis=0)
           for x, y, z, q, e in zip(aw, kdt, a, qs, e_pos)]
    cdec = [jnp.concatenate([jnp.exp(end)] * (DN_DV // LANES), axis=1) for end in ends]
    return lhs, u, cdec


def _delta_scan_kernel(qf_ref, kf_ref, vf_ref, gf_ref, gtf_ref, qb_ref, kb_ref, vb_ref, gb_ref, gtb_ref,
                       s0_ref, m_ref, mm_ref, of_ref, ob_ref, sfin_ref, s_scr, *, nc, par, hpg):
    step = pl.program_id(2)

    @pl.when(step == 0)
    def _():
        s_scr[...] = s0_ref[0]

    fwd_refs = (qf_ref, kf_ref, vf_ref, gf_ref, gtf_ref, of_ref)
    bwd_refs = (qb_ref, kb_ref, vb_ref, gb_ref, gtb_ref, ob_ref)

    def body(it, carry):
        chains = []
        for hh in range(hpg):
            head = pl.program_id(1) * hpg + hh
            for rev in (0, 1):
                for p in range(par):
                    c = nc - 1 - it * par - p if rev else it * par + p
                    chains.append((hh, head, rev, pl.ds(pl.multiple_of(c * CHUNK, CHUNK), CHUNK)))

        def tile(slot, width):
            return [(bwd_refs if rev else fwd_refs)[slot][0, rows, hh * width:(hh + 1) * width]
                    for hh, _, rev, rows in chains]

        gates = [(bwd_refs if rev else fwd_refs)[3][0, rows, :] for _, _, rev, rows in chains]
        gc_rows = [(bwd_refs if rev else fwd_refs)[4][
            0, pl.ds(pl.multiple_of((rev * DN_HEADS + head) * SUBLANES, SUBLANES), SUBLANES), rows][0:1]
            for _, head, rev, rows in chains]
        lhs, u, cdec = _delta_prep_group(tile(0, DN_DK), tile(1, DN_DK), tile(2, DN_DV), gates, gc_rows,
                                         [rev for _, _, rev, _ in chains], [head for _, head, _, _ in chains],
                                         m_ref, mm_ref)
        seqs = [(hh, rev) for hh in range(hpg) for rev in (0, 1)]
        state = [s_scr[hh, rev] for hh, rev in seqs]
        for p in range(par):
            idx = [(hh * 2 + rev) * par + p for hh, rev in seqs]
            res = [jnp.dot(lhs[n], jnp.concatenate([s.astype(BF16), u[n]], axis=0), preferred_element_type=F32)
                   for s, n in zip(state, idx)]
            state = [cdec[n] * s + r[:DN_DK] for s, r, n in zip(state, res, idx)]
            for r, n in zip(res, idx):
                hh, _, rev, rows = chains[n]
                o_ref = (bwd_refs if rev else fwd_refs)[5]
                o_ref[0, rows, hh * DN_DV:(hh + 1) * DN_DV] = r[DN_DK:].astype(o_ref.dtype)
        for (hh, rev), s in zip(seqs, state):
            s_scr[hh, rev] = s
        return carry

    lax.fori_loop(0, nc // par, body, 0)

    @pl.when(step == pl.num_programs(2) - 1)
    def _():
        sfin_ref[0] = s_scr[...]


def _delta_scan(qkv, gates, gates_t, s0, masks, merge_masks):
    bsz, t, _ = qkv.shape
    tb = min(t, 1024)
    ns = t // tb
    hpg = DELTA_HEADS_PER_STEP
    groups = DN_HEADS // hpg

    def fwd(width, off):
        return pl.BlockSpec((1, tb, hpg * width), lambda b, h, s: (b, s, off + h))

    def bwd(width, off):
        return pl.BlockSpec((1, tb, hpg * width), lambda b, h, s: (b, ns - 1 - s, off + h))

    state_spec = pl.BlockSpec((1, hpg, 2, DN_DK, DN_DV), lambda b, h, s: (b, h, 0, 0, 0))
    o_shape = jax.ShapeDtypeStruct((bsz, t, DN_V), BF16)
    return pl.pallas_call(
        functools.partial(_delta_scan_kernel, nc=tb // CHUNK, par=min(SCAN_PAR, tb // CHUNK), hpg=hpg),
        grid=(bsz, groups, ns),
        in_specs=[fwd(DN_DK, 0), fwd(DN_DK, groups), fwd(DN_DV, groups),
                  pl.BlockSpec((1, tb, LANES), lambda b, h, s: (b, s, 0)),
                  pl.BlockSpec((1, LANES, tb), lambda b, h, s: (b, 0, s)),
                  bwd(DN_DK, 0), bwd(DN_DK, groups), bwd(DN_DV, groups),
                  pl.BlockSpec((1, tb, LANES), lambda b, h, s: (b, ns - 1 - s, 0)),
                  pl.BlockSpec((1, LANES, tb), lambda b, h, s: (b, 0, ns - 1 - s)),
                  state_spec,
                  pl.BlockSpec(masks.shape, lambda b, h, s: (0, 0, 0)),
                  pl.BlockSpec(merge_masks.shape, lambda b, h, s: (0, 0, 0))],
        out_specs=[fwd(DN_DV, 0), bwd(DN_DV, 0), state_spec],
        out_shape=[o_shape, o_shape, jax.ShapeDtypeStruct(s0.shape, F32)],
        scratch_shapes=[pltpu.VMEM((hpg, 2, DN_DK, DN_DV), F32)],
        compiler_params=_params(("parallel", "parallel", "arbitrary")),
        name="delta_scan",
    )(qkv, qkv, qkv, gates, gates_t, qkv, qkv, qkv, gates, gates_t, s0, masks, merge_masks)


def _gla_prep_group(qs, ks, lgs, revs, m_ref):
    in_hi = lax.broadcasted_iota(jnp.int32, (CHUNK, GLA_DK), 0) >= SUB
    bc = [_masked_sums(m_ref[2 * rev], lg) for lg, rev in zip(lgs, revs)]

    def per_block(x, rows):
        return jnp.concatenate([jnp.broadcast_to(x[r:r + 1], (SUB, GLA_DK)) for r in rows], axis=0)

    ref_rows = [(SUB // 2 - 1, SUB + SUB // 2 - 1) if rev else (SUB // 2, SUB + SUB // 2) for rev in revs]
    last_rows = [(0, SUB) if rev else (SUB - 1, CHUNK - 1) for rev in revs]
    refb = [per_block(x, rows) for x, rows in zip(bc, ref_rows)]
    lastb = [per_block(x, rows) for x, rows in zip(bc, last_rows)]
    qf = [q.astype(F32) * (GLA_DK ** -0.5) for q in qs]
    kf = [k.astype(F32) for k in ks]
    qe = [q * jnp.exp(x - r) for q, x, r in zip(qf, bc, refb)]
    ke = [k * jnp.exp(r - x) for k, x, r in zip(kf, bc, refb)]
    qd = [q * jnp.exp(x) for q, x in zip(qf, bc)]
    kd = [k * jnp.exp(l - x) for k, x, l in zip(kf, bc, lastb)]
    a_in = [_mm_nt(x, y) for x, y in zip(qe, ke)]
    a_x = [_mm_nt(x, y) for x, y in zip(qd, kd)]
    a = [x * m_ref[2 * rev] + y * m_ref[2 * rev + 1] for x, y, rev in zip(a_in, a_x, revs)]
    last1 = [x[rows[1]:rows[1] + 1] if rev else x[rows[0]:rows[0] + 1] for x, rows, rev in zip(bc, last_rows, revs)]
    last2 = [x[rows[0]:rows[0] + 1] if rev else x[rows[1]:rows[1] + 1] for x, rows, rev in zip(bc, last_rows, revs)]
    second = [jnp.logical_not(in_hi) if rev else in_hi for rev in revs]
    qd2 = [x * jnp.where(sec, jnp.exp(l1), 1.0) for x, sec, l1 in zip(qd, second, last1)]
    kd2 = [x * jnp.where(sec, 1.0, jnp.exp(l2)) for x, sec, l2 in zip(kd, second, last2)]
    cdcol = [jnp.broadcast_to(jnp.exp(l1 + l2), (GLA_DK, LANES)).T for l1, l2 in zip(last1, last2)]
    lhs = [jnp.concatenate([x.astype(BF16), y.astype(BF16)], axis=1) for x, y in zip(qd2, a)]
    kdt = [x.T.astype(BF16) for x in kd2]
    decay = [jnp.concatenate([x] * (GLA_DV // LANES), axis=1) for x in cdcol]
    return lhs, kdt, decay


def _gla_scan_kernel(qf_ref, kf_ref, vf_ref, gf_ref, qb_ref, kb_ref, vb_ref, gb_ref, s0_ref, m_ref,
                     of_ref, ob_ref, sfin_ref, s_scr, *, nc, par):
    step = pl.program_id(2)

    @pl.when(step == 0)
    def _():
        s_scr[...] = s0_ref[0, 0]

    def body(it, carry):
        rows = ([pl.ds(pl.multiple_of((it * par + p) * CHUNK, CHUNK), CHUNK) for p in range(par)]
                + [pl.ds(pl.multiple_of((nc - 1 - it * par - p) * CHUNK, CHUNK), CHUNK) for p in range(par)])
        refs = [(qf_ref, kf_ref, vf_ref, gf_ref, of_ref)] * par + [(qb_ref, kb_ref, vb_ref, gb_ref, ob_ref)] * par
        revs = [0] * par + [1] * par
        lhs, kdt, decay = _gla_prep_group([r[0][0, rw, :] for r, rw in zip(refs, rows)],
                                          [r[1][0, rw, :] for r, rw in zip(refs, rows)],
                                          [r[3][0, rw, :] for r, rw in zip(refs, rows)], revs, m_ref)
        vs = [r[2][0, rw, :] for r, rw in zip(refs, rows)]
        upd = [jnp.dot(x, v, preferred_element_type=F32) for x, v in zip(kdt, vs)]
        state = [s_scr[0], s_scr[1]]
        for p in range(par):
            idx = [p, par + p]
            outs = [jnp.dot(lhs[n], jnp.concatenate([state[d].astype(BF16), vs[n]], axis=0),
                            preferred_element_type=F32) for d, n in enumerate(idx)]
            state = [decay[n] * state[d] + upd[n] for d, n in enumerate(idx)]
            for d, n in enumerate(idx):
                refs[n][4][0, rows[n], :] = outs[d].astype(refs[n][4].dtype)
        s_scr[0] = state[0]
        s_scr[1] = state[1]
        return carry

    lax.fori_loop(0, nc // par, body, 0)

    @pl.when(step == pl.num_programs(2) - 1)
    def _():
        sfin_ref[0, 0] = s_scr[...]


def _gla_scan(p, lg, s0, masks):
    bsz, t, _ = p.shape
    tb = min(t, 1024)
    ns = t // tb
    q_off = GLA_V // GLA_DK
    k_off = q_off + GLA_HEADS
    v_off = (GLA_V + 2 * GLA_K) // GLA_DV

    def spec(width, off, rev):
        if rev:
            return pl.BlockSpec((1, tb, width), lambda b, h, s: (b, ns - 1 - s, off + h))
        return pl.BlockSpec((1, tb, width), lambda b, h, s: (b, s, off + h))

    state_spec = pl.BlockSpec((1, 1, 2, GLA_DK, GLA_DV), lambda b, h, s: (b, h, 0, 0, 0))
    o_shape = jax.ShapeDtypeStruct((bsz, t, GLA_V), BF16)
    return pl.pallas_call(
        functools.partial(_gla_scan_kernel, nc=tb // CHUNK, par=min(GLA_SCAN_PAR, tb // CHUNK)),
        grid=(bsz, GLA_HEADS, ns),
        in_specs=[spec(GLA_DK, q_off, 0), spec(GLA_DK, k_off, 0), spec(GLA_DV, v_off, 0), spec(GLA_DK, 0, 0),
                  spec(GLA_DK, q_off, 1), spec(GLA_DK, k_off, 1), spec(GLA_DV, v_off, 1),
                  spec(GLA_DK, GLA_HEADS, 1),
                  state_spec,
                  pl.BlockSpec(masks.shape, lambda b, h, s: (0, 0, 0))],
        out_specs=[spec(GLA_DV, 0, 0), spec(GLA_DV, 0, 1), state_spec],
        out_shape=[o_shape, o_shape, jax.ShapeDtypeStruct(s0.shape, F32)],
        scratch_shapes=[pltpu.VMEM((2, GLA_DK, GLA_DV), F32)],
        compiler_params=_params(("parallel", "parallel", "arbitrary")),
        name="gla_scan",
    )(p, p, p, lg, p, p, p, lg, s0, masks)


def _gated_out_kernel(of_ref, ob_ref, z_ref, x_ref, gt_ref, ng_ref, w_ref, fg_ref, o_ref, y_scr, *, dv, final):
    tm, dvt = of_ref.shape[1:]
    for r0 in range(0, tm, CHUNK):
        rows = slice(r0, r0 + CHUNK)
        for s in range(dvt // dv):
            cols = slice(s * dv, (s + 1) * dv)
            seg = of_ref[0, rows, cols].astype(F32) + ob_ref[0, rows, cols].astype(F32)
            ms = jnp.mean(seg * seg, axis=-1, keepdims=True)
            y = seg * lax.rsqrt(ms + EPS) * ng_ref[:, cols] * _silu(z_ref[0, rows, cols].astype(F32))
            y_scr[rows, cols] = y.astype(BF16)
        proj = jnp.dot(y_scr[rows, :], w_ref[...], preferred_element_type=F32)
        xn = x_ref[0, rows, :] + gt_ref[0] * proj
        if final:
            ms = jnp.mean(xn * xn, axis=-1, keepdims=True)
            xn = xn * lax.rsqrt(ms + EPS) * fg_ref[...]
        o_ref[0, rows, :] = xn


def _gated_out(o_f, o_b, p, z_block, x, gt, norm_g_row, w_out, final_g_row, dv, final):
    bsz, t, d = x.shape
    dvt = o_f.shape[2]
    tm = min(t, 512)
    const = lambda b, i: (0, 0)
    return pl.pallas_call(
        functools.partial(_gated_out_kernel, dv=dv, final=final),
        grid=(bsz, t // tm),
        in_specs=[pl.BlockSpec((1, tm, dvt), lambda b, i: (b, i, 0)),
                  pl.BlockSpec((1, tm, dvt), lambda b, i: (b, i, 0)),
                  pl.BlockSpec((1, tm, dvt), lambda b, i: (b, i, z_block)),
                  pl.BlockSpec((1, tm, d), lambda b, i: (b, i, 0)),
                  pl.BlockSpec((1, 1, d), lambda b, i: (b, 0, 0)),
                  pl.BlockSpec((1, dvt), const),
                  pl.BlockSpec((dvt, d), const),
                  pl.BlockSpec((1, d), const)],
        out_specs=pl.BlockSpec((1, tm, d), lambda b, i: (b, i, 0)),
        out_shape=jax.ShapeDtypeStruct((bsz, t, d), F32),
        scratch_shapes=[pltpu.VMEM((tm, dvt), BF16)],
        compiler_params=_params(("parallel", "parallel")),
        name="gated_out_final" if final else "gated_out",
    )(o_f, o_b, p, x, gt, norm_g_row, w_out, final_g_row)


def _pad_cols(w, width):
    return jnp.pad(w, ((0, 0), (0, width - w.shape[1])))


def kernel(x, c, ctx, c_ctx, mod_w, mod_b, norm_g, dn_w_in, dn_conv_w, dn_a_log, dn_dt_bias, dn_norm_g,
           dn_w_out, gla_w_in, gla_w_g2, gla_b_g, gla_norm_g, gla_w_out, final_g):
    bsz, _, d = x.shape
    assert bsz + 1 <= 8
    cond = jnp.concatenate([c, c_ctx[None], jnp.zeros((8 - bsz - 1, d), F32)], axis=0)
    mod = _modulation(cond, mod_w, mod_b)

    def mod_rows(layer):
        lat = [mod[layer, :bsz, None, n * d:(n + 1) * d] for n in range(3)]
        cx = [jnp.broadcast_to(mod[layer, bsz, n * d:(n + 1) * d], (bsz, 1, d)) for n in range(3)]
        return lat, cx

    final_row = final_g.reshape(1, d)

    (sh_l, sc_l, gt_l), (sh_c, sc_c, gt_c) = mod_rows(0)
    w0 = dn_w_in[0].astype(BF16)
    w0_main, w0_small = w0[:, :DN_MAIN], _pad_cols(w0[:, DN_MAIN:], LANES)
    gate_params = jnp.zeros((8, LANES), F32)
    gate_params = gate_params.at[0, :2 * DN_HEADS].set(dn_a_log[0].reshape(-1))
    gate_params = gate_params.at[1, :2 * DN_HEADS].set(dn_dt_bias[0].reshape(-1))
    g0 = norm_g[0].reshape(1, d)
    conv_w9 = dn_conv_w[0].reshape(9, DN_QKV)
    dmasks = (_delta_mask_array(), _delta_merge_mask_array())
    dn_g_row = jnp.tile(dn_norm_g[0], DN_HEADS).reshape(1, DN_V)
    dn_wo = dn_w_out[0].astype(BF16)

    p_c, gates_c, gates_tc = _inproj(_dn_inproj_kernel, "dn_inproj", ctx, sh_c, sc_c, g0, w0_main, w0_small,
                                     [gate_params], LANES, aux_transposed=True)
    qkv_c = _conv_seq(p_c, conv_w9)
    qkv_l, z_l, gates_l, gates_tl = _dn_inproj_conv(x, sh_l, sc_l, g0, w0_main, w0_small, gate_params, conv_w9)
    s_zero = jnp.zeros((bsz, DN_HEADS, 2, DN_DK, DN_DV), F32)
    oc_f, oc_b, s_ctx = _delta_scan(qkv_c, gates_c, gates_tc, s_zero, *dmasks)
    ol_f, ol_b, _ = _delta_scan(qkv_l, gates_l, gates_tl, s_ctx, *dmasks)
    ctx1 = _gated_out(oc_f, oc_b, p_c, DN_QKV // DN_V, ctx, gt_c, dn_g_row, dn_wo, final_row, DN_DV, False)
    x1 = _gated_out(ol_f, ol_b, z_l, 0, x, gt_l, dn_g_row, dn_wo, final_row, DN_DV, False)

    (sh_l, sc_l, gt_l), (sh_c, sc_c, _) = mod_rows(1)
    w1 = gla_w_in[0].astype(BF16)
    w1_main = jnp.concatenate([w1[:, 2 * GLA_K + GLA_V:GLA_MAIN], w1[:, :2 * GLA_K + GLA_V]], axis=1)
    w1_small = _pad_cols(w1[:, GLA_MAIN:], LANES)
    wg = jnp.zeros((LANES, 2 * GLA_K), F32)
    wg = wg.at[:GLA_RANK, :GLA_K].set(gla_w_g2[0, 0]).at[GLA_RANK:2 * GLA_RANK, GLA_K:].set(gla_w_g2[0, 1])
    wg = wg.astype(BF16)
    bg = gla_b_g[0].reshape(1, 2 * GLA_K)
    g1 = norm_g[1].reshape(1, d)
    gmasks = _gla_mask_array()
    gla_g_row = jnp.tile(gla_norm_g[0], GLA_HEADS).reshape(1, GLA_V)
    gla_wo = gla_w_out[0].astype(BF16)

    def gla_features(u, sh, sc):
        return _inproj(_gla_inproj_kernel, "gla_inproj", u, sh, sc, g1, w1_main, w1_small, [wg, bg], 2 * GLA_K)

    pg_c, lg_c = gla_features(ctx1, sh_c, sc_c)
    pg_l, lg_l = gla_features(x1, sh_l, sc_l)
    s_zero = jnp.zeros((bsz, GLA_HEADS, 2, GLA_DK, GLA_DV), F32)
    _, _, s_ctx = _gla_scan(pg_c, lg_c, s_zero, gmasks)
    og_f, og_b, _ = _gla_scan(pg_l, lg_l, s_ctx, gmasks)
    return _gated_out(og_f, og_b, pg_l, 0, x1, gt_l, gla_g_row, gla_wo, final_row, GLA_DV, True)
```

```python
import functools

import numpy as np
import jax
import jax.numpy as jnp
from jax import lax
from jax.experimental import pallas as pl
from jax.experimental.pallas import tpu as pltpu

F32, BF16 = jnp.float32, jnp.bfloat16
HIGHEST = lax.Precision.HIGHEST
EPS = 1e-6

LANES = 128
SUBLANES = 8
GRID_W = 64
CHUNK = 128
SUB = 64
SCAN_PAR = 4
GLA_SCAN_PAR = 4
DELTA_HEADS_PER_STEP = 2
VMEM_LIMIT_BYTES = 52 * 1024 * 1024
FUSED_VMEM_LIMIT_BYTES = 58 * 1024 * 1024

DN_DK, DN_HEADS, DN_DV = 128, 8, 256
CONV_K = 3
DN_QK = DN_HEADS * DN_DK
DN_V = DN_HEADS * DN_DV
DN_QKV = 2 * DN_QK + DN_V
DN_MAIN = DN_QKV + DN_V
GLA_HEADS, GLA_DK, GLA_DV, GLA_RANK, GLA_TAU = 4, 128, 512, 16, 16.0
GLA_K = GLA_HEADS * GLA_DK
GLA_V = GLA_HEADS * GLA_DV
GLA_MAIN = 2 * GLA_K + 2 * GLA_V


def _sigmoid(x):
    return 1.0 / (1.0 + jnp.exp(-x))


def _silu(x):
    return x * _sigmoid(x)


def _softplus(x):
    return jnp.maximum(x, 0.0) + jnp.log(1.0 + jnp.exp(-jnp.abs(x)))


def _mm(a, b):
    return jnp.dot(a.astype(BF16), b.astype(BF16), preferred_element_type=F32)


def _mm_nt(a, b):
    return lax.dot_general(a.astype(BF16), b.astype(BF16), (((1,), (1,)), ((), ())),
                           preferred_element_type=F32)


def _masked_sums(mask01, x):
    hi = x.astype(BF16)
    r1 = x - hi.astype(F32)
    mid = r1.astype(BF16)
    lo = (r1 - mid.astype(F32)).astype(BF16)
    n = x.shape[1]
    s = jnp.dot(mask01.astype(BF16), jnp.concatenate([hi, mid, lo], axis=1), preferred_element_type=F32)
    return (s[:, :n] + s[:, n:2 * n]) + s[:, 2 * n:]


def _mm_f32(a, b):
    return jnp.dot(a, b, precision=HIGHEST, preferred_element_type=F32)


def _params(semantics):
    return pltpu.CompilerParams(dimension_semantics=semantics, vmem_limit_bytes=VMEM_LIMIT_BYTES)


def _mod_kernel(c_ref, w_ref, b_ref, o_ref):
    o_ref[0] = _mm_f32(_silu(c_ref[...]), w_ref[0]) + b_ref[0]


def _modulation(cond, mod_w, mod_b):
    depth, d, d3 = mod_w.shape
    tn = d
    return pl.pallas_call(
        _mod_kernel,
        grid=(depth, d3 // tn),
        in_specs=[pl.BlockSpec((8, d), lambda l, j: (0, 0)),
                  pl.BlockSpec((1, d, tn), lambda l, j: (l, 0, j)),
                  pl.BlockSpec((1, 1, tn), lambda l, j: (l, 0, j))],
        out_specs=pl.BlockSpec((1, 8, tn), lambda l, j: (l, 0, j)),
        out_shape=jax.ShapeDtypeStruct((depth, 8, d3), F32),
        compiler_params=_params(("parallel", "parallel")),
        name="modulation",
    )(cond, mod_w, mod_b.reshape(depth, 1, d3))


NORM_STRIP = 32


def _norm_mod_to(h_scr, x_ref, g_ref, sc_ref, sh_ref, dst_row=0, valid=None):
    gain = g_ref[...] * (1.0 + sc_ref[0])
    shift = sh_ref[0]
    n_strips = x_ref.shape[1] // NORM_STRIP

    def strip(n, carry):
        x = x_ref[0, pl.ds(pl.multiple_of(n * NORM_STRIP, NORM_STRIP), NORM_STRIP), :]
        ms = jnp.mean(x * x, axis=-1, keepdims=True)
        h = (x * lax.rsqrt(ms + EPS)) * gain + shift
        if valid is not None:
            h = jnp.where(valid, h, 0.0)
        h_scr[pl.ds(pl.multiple_of(dst_row + n * NORM_STRIP, NORM_STRIP), NORM_STRIP), :] = h.astype(BF16)
        return carry

    lax.fori_loop(0, n_strips, strip, 0, unroll=min(8, n_strips))


def _dn_gates(hb, ws_ref, ap_ref, gates_ref, gates_t_ref):
    raw = jnp.dot(hb, ws_ref[...], preferred_element_type=F32)
    log_decay = -jnp.exp(ap_ref[0:1, :]) * _softplus(raw + ap_ref[1:2, :])
    beta = _sigmoid(raw)
    r = lax.broadcasted_iota(jnp.int32, (2 * CHUNK, CHUNK), 0)
    c = lax.broadcasted_iota(jnp.int32, (2 * CHUNK, CHUNK), 1)
    tri2 = (jnp.where(r < CHUNK, r - c, c - r + CHUNK) >= 0).astype(F32)
    lane = lax.broadcasted_iota(jnp.int32, (CHUNK, LANES), 1)
    for n in range(raw.shape[0] // CHUNK):
        rows = slice(n * CHUNK, (n + 1) * CHUNK)
        cs = _masked_sums(tri2, log_decay[rows])
        gt = jnp.where(lane < DN_HEADS, cs[:CHUNK], jnp.where(lane < 2 * DN_HEADS, cs[CHUNK:], beta[rows]))
        gates_ref[0, rows, :] = gt
        gtt = gt.T
        gates_t_ref[0, :, rows] = jnp.concatenate(
            [jnp.broadcast_to(gtt[m:m + 1], (SUBLANES, CHUNK)) for m in range(2 * DN_HEADS)], axis=0)


def _dn_inproj_kernel(x_ref, sh_ref, sc_ref, g_ref, w_ref, ws_ref, ap_ref, o_ref, gates_ref, gates_t_ref, h_scr):
    @pl.when(pl.program_id(2) == 0)
    def _():
        _norm_mod_to(h_scr, x_ref, g_ref, sc_ref, sh_ref)
        _dn_gates(h_scr[...], ws_ref, ap_ref, gates_ref, gates_t_ref)

    o_ref[0] = jnp.dot(h_scr[...], w_ref[...], preferred_element_type=F32).astype(o_ref.dtype)


def _conv_strip(up, mid, down, w, first, last, scale):
    accs = [w[dc] * up + w[3 + dc] * mid + w[6 + dc] * down for dc in range(3)]
    y = _silu(accs[1] + jnp.where(first, 0.0, pltpu.roll(accs[0], 1, 0))
              + jnp.where(last, 0.0, pltpu.roll(accs[2], GRID_W - 1, 0)))
    if scale is not None:
        y = y * (lax.rsqrt(jnp.sum(y * y, axis=-1, keepdims=True) + EPS) * scale)
    return y


def _dn_inproj_conv_kernel(xp_ref, x_ref, xn_ref, sh_ref, sc_ref, g_ref, w_ref, ws_ref, ap_ref, cw_ref,
                           qkv_ref, z_ref, gates_ref, gates_t_ref, h_scr, p_scr, *, tw):
    i = pl.program_id(1)
    tm = x_ref.shape[1]
    n_rows = tm // GRID_W
    _norm_mod_to(h_scr, xp_ref, g_ref, sc_ref, sh_ref, 0, i > 0)
    _norm_mod_to(h_scr, x_ref, g_ref, sc_ref, sh_ref, GRID_W)
    _norm_mod_to(h_scr, xn_ref, g_ref, sc_ref, sh_ref, GRID_W + tm, i < pl.num_programs(1) - 1)
    _dn_gates(h_scr[GRID_W:GRID_W + tm, :], ws_ref, ap_ref, gates_ref, gates_t_ref)

    pos = lax.broadcasted_iota(jnp.int32, (GRID_W, DN_DK), 0)
    first, last = pos == 0, pos == GRID_W - 1
    n_conv = DN_QKV // tw
    n_z = DN_V // tw

    def conv_tile(t):
        cols0 = t * tw
        scale = DN_DK ** -0.5 if cols0 < DN_QK else (1.0 if cols0 < 2 * DN_QK else None)
        for s in range(tw // DN_DK):
            lanes = slice(s * DN_DK, (s + 1) * DN_DK)
            w = [cw_ref[k:k + 1, cols0 + s * DN_DK:cols0 + (s + 1) * DN_DK] for k in range(9)]
            up = p_scr[t % 2, 0:GRID_W, lanes]
            mid = p_scr[t % 2, GRID_W:2 * GRID_W, lanes]
            for r in range(n_rows):
                down = p_scr[t % 2, (r + 2) * GRID_W:(r + 3) * GRID_W, lanes]
                y = _conv_strip(up, mid, down, w, first, last, scale)
                qkv_ref[0, r * GRID_W:(r + 1) * GRID_W, cols0 + s * DN_DK:cols0 + (s + 1) * DN_DK] = (
                    y.astype(qkv_ref.dtype))
                up, mid = mid, down

    for t in range(n_conv):
        p_scr[t % 2] = jnp.dot(h_scr[...], w_ref[:, t * tw:(t + 1) * tw], preferred_element_type=F32)
        if t > 0:
            conv_tile(t - 1)
    for t in range(n_z):
        z_ref[0, :, t * tw:(t + 1) * tw] = jnp.dot(
            h_scr[GRID_W:GRID_W + tm, :], w_ref[:, DN_QKV + t * tw:DN_QKV + (t + 1) * tw],
            preferred_element_type=F32).astype(z_ref.dtype)
        if t == 0:
            conv_tile(n_conv - 1)


def _dn_inproj_conv(x, sh, sc, g, w_main, w_small, gate_params, conv_w9):
    bsz, t, d = x.shape
    tm = 512
    tw = 1024
    rows_per_tile = tm // GRID_W
    n_rows = t // GRID_W
    const = lambda b, i: (0, 0)
    return pl.pallas_call(
        functools.partial(_dn_inproj_conv_kernel, tw=tw),
        grid=(bsz, t // tm),
        in_specs=[pl.BlockSpec((1, GRID_W, d), lambda b, i: (b, jnp.maximum(i * rows_per_tile - 1, 0), 0)),
                  pl.BlockSpec((1, tm, d), lambda b, i: (b, i, 0)),
                  pl.BlockSpec((1, GRID_W, d), lambda b, i: (b, jnp.minimum((i + 1) * rows_per_tile, n_rows - 1), 0)),
                  pl.BlockSpec((1, 1, d), lambda b, i: (b, 0, 0)),
                  pl.BlockSpec((1, 1, d), lambda b, i: (b, 0, 0)),
                  pl.BlockSpec((1, d), const),
                  pl.BlockSpec(w_main.shape, const),
                  pl.BlockSpec((d, LANES), const),
                  pl.BlockSpec(gate_params.shape, const),
                  pl.BlockSpec(conv_w9.shape, const)],
        out_specs=[pl.BlockSpec((1, tm, DN_QKV), lambda b, i: (b, i, 0)),
                   pl.BlockSpec((1, tm, DN_V), lambda b, i: (b, i, 0)),
                   pl.BlockSpec((1, tm, LANES), lambda b, i: (b, i, 0)),
                   pl.BlockSpec((1, LANES, tm), lambda b, i: (b, 0, i))],
        out_shape=[jax.ShapeDtypeStruct((bsz, t, DN_QKV), BF16), jax.ShapeDtypeStruct((bsz, t, DN_V), BF16),
                   jax.ShapeDtypeStruct((bsz, t, LANES), F32), jax.ShapeDtypeStruct((bsz, LANES, t), F32)],
        scratch_shapes=[pltpu.VMEM((tm + 2 * GRID_W, d), BF16), pltpu.VMEM((2, tm + 2 * GRID_W, tw), F32)],
        compiler_params=pltpu.CompilerParams(dimension_semantics=("parallel", "parallel"),
                                             vmem_limit_bytes=FUSED_VMEM_LIMIT_BYTES),
        name="dn_inproj_conv",
    )(x, x, x, sh, sc, g, w_main, w_small, gate_params, conv_w9)


def _gla_inproj_kernel(x_ref, sh_ref, sc_ref, g_ref, w_ref, ws_ref, wg_ref, bg_ref, o_ref, lg_ref, h_scr, *, tw):
    _norm_mod_to(h_scr, x_ref, g_ref, sc_ref, sh_ref)
    hb = h_scr[...]
    glb = jnp.dot(hb, ws_ref[...], preferred_element_type=F32).astype(BF16)
    n_tiles = w_ref.shape[1] // tw
    for t in range(n_tiles):
        o_ref[0, :, t * tw:(t + 1) * tw] = jnp.dot(hb, w_ref[:, t * tw:(t + 1) * tw],
                                                   preferred_element_type=F32).astype(o_ref.dtype)
        for r0 in range(t * CHUNK, glb.shape[0], n_tiles * CHUNK):
            zg = jnp.dot(glb[r0:r0 + CHUNK], wg_ref[...], preferred_element_type=F32) + bg_ref[...]
            lg_ref[0, r0:r0 + CHUNK, :] = -_softplus(-zg) * (1.0 / GLA_TAU)


def _gla_inproj(x, sh, sc, g, w_main, w_small, wg, bg):
    bsz, t, d = x.shape
    n = w_main.shape[1]
    tm = min(t, 512)
    const = lambda b, i: (0, 0)
    return pl.pallas_call(
        functools.partial(_gla_inproj_kernel, tw=1024),
        grid=(bsz, t // tm),
        in_specs=[pl.BlockSpec((1, tm, d), lambda b, i: (b, i, 0)),
                  pl.BlockSpec((1, 1, d), lambda b, i: (b, 0, 0)),
                  pl.BlockSpec((1, 1, d), lambda b, i: (b, 0, 0)),
                  pl.BlockSpec((1, d), const),
                  pl.BlockSpec(w_main.shape, const),
                  pl.BlockSpec((d, LANES), const),
                  pl.BlockSpec(wg.shape, const),
                  pl.BlockSpec(bg.shape, const)],
        out_specs=[pl.BlockSpec((1, tm, n), lambda b, i: (b, i, 0)),
                   pl.BlockSpec((1, tm, 2 * GLA_K), lambda b, i: (b, i, 0))],
        out_shape=[jax.ShapeDtypeStruct((bsz, t, n), BF16), jax.ShapeDtypeStruct((bsz, t, 2 * GLA_K), F32)],
        scratch_shapes=[pltpu.VMEM((tm, d), BF16)],
        compiler_params=_params(("parallel", "parallel")),
        name="gla_inproj",
    )(x, sh, sc, g, w_main, w_small, wg, bg)


def _inproj(kernel_fn, name, x, sh, sc, g, w_main, w_small, extra, aux_width, aux_transposed=False):
    bsz, t, d = x.shape
    n = w_main.shape[1]
    tm = min(t, 1024)
    tn = 1024
    const = lambda b, i, j: (0, 0)
    extra_specs = [pl.BlockSpec(e.shape, const) for e in extra]
    out_specs = [pl.BlockSpec((1, tm, tn), lambda b, i, j: (b, i, j)),
                 pl.BlockSpec((1, tm, aux_width), lambda b, i, j: (b, i, 0))]
    out_shape = [jax.ShapeDtypeStruct((bsz, t, n), BF16), jax.ShapeDtypeStruct((bsz, t, aux_width), F32)]
    if aux_transposed:
        out_specs.append(pl.BlockSpec((1, aux_width, tm), lambda b, i, j: (b, 0, i)))
        out_shape.append(jax.ShapeDtypeStruct((bsz, aux_width, t), F32))
    return pl.pallas_call(
        kernel_fn,
        grid=(bsz, t // tm, n // tn),
        in_specs=[pl.BlockSpec((1, tm, d), lambda b, i, j: (b, i, 0)),
                  pl.BlockSpec((1, 1, d), lambda b, i, j: (b, 0, 0)),
                  pl.BlockSpec((1, 1, d), lambda b, i, j: (b, 0, 0)),
                  pl.BlockSpec((1, d), const),
                  pl.BlockSpec((d, tn), lambda b, i, j: (0, j)),
                  pl.BlockSpec((d, LANES), const)] + extra_specs,
        out_specs=out_specs,
        out_shape=out_shape,
        scratch_shapes=[pltpu.VMEM((tm, d), BF16)],
        compiler_params=_params(("parallel", "parallel", "arbitrary")),
        name=name,
    )(x, sh, sc, g, w_main, w_small, *extra)


def _conv_seq_kernel(p_ref, w_ref, o_ref, *, n_q_tiles, n_qk_tiles):
    j = pl.program_id(1)
    _, t, tc = p_ref.shape
    pos = lax.broadcasted_iota(jnp.int32, (t, DN_DK), 0)
    first, last = pos == 0, pos == t - 1

    def run(normalise):
        scale = jnp.where(j < n_q_tiles, DN_DK ** -0.5, 1.0)
        for s in range(tc // DN_DK):
            lanes = slice(s * DN_DK, (s + 1) * DN_DK)
            x = p_ref[0, :, lanes].astype(F32)
            w = [w_ref[CONV_K + dc:CONV_K + dc + 1, lanes] for dc in range(CONV_K)]
            y = _silu(w[1] * x + jnp.where(first, 0.0, pltpu.roll(w[0] * x, 1, 0))
                      + jnp.where(last, 0.0, pltpu.roll(w[2] * x, t - 1, 0)))
            if normalise:
                y = y * (lax.rsqrt(jnp.sum(y * y, axis=-1, keepdims=True) + EPS) * scale)
            o_ref[0, :, lanes] = y.astype(o_ref.dtype)

    @pl.when(j < n_qk_tiles)
    def _():
        run(True)

    @pl.when(j >= n_qk_tiles)
    def _():
        run(False)


def _conv_seq(p, conv_w9):
    bsz, t, _ = p.shape
    tc = 512
    kern = functools.partial(_conv_seq_kernel, n_q_tiles=DN_QK // tc, n_qk_tiles=2 * DN_QK // tc)
    return pl.pallas_call(
        kern,
        grid=(bsz, DN_QKV // tc),
        in_specs=[pl.BlockSpec((1, t, tc), lambda b, j: (b, 0, j)),
                  pl.BlockSpec((CONV_K * CONV_K, tc), lambda b, j: (0, j))],
        out_specs=pl.BlockSpec((1, t, tc), lambda b, j: (b, 0, j)),
        out_shape=jax.ShapeDtypeStruct((bsz, t, DN_QKV), BF16),
        compiler_params=_params(("parallel", "parallel")),
        name="conv_seq",
    )(p, conv_w9)


_BASE_BLOCK = 8
_M_EYE, _M_DIAG = 4, 5
_MERGE_BLOCKS = tuple(_BASE_BLOCK << n for n in range((CHUNK // _BASE_BLOCK).bit_length() - 1))


def _half_rows(x, b, half):
    return jnp.concatenate([x[s * b:(s + 1) * b] for s in range(half, x.shape[0] // b, 2)], axis=0)


def _put_half_rows(x, upd, b, half):
    return jnp.concatenate([upd[(s // 2) * b:(s // 2 + 1) * b] if s % 2 == half else x[s * b:(s + 1) * b]
                            for s in range(x.shape[0] // b)], axis=0)


def _delta_mask_array():
    r = np.arange(CHUNK)[:, None]
    c = np.arange(CHUNK)[None, :]
    mats = [r >= c, r > c, r <= c, r < c, r == c, (r // _BASE_BLOCK) == (c // _BASE_BLOCK)]
    return jnp.asarray(np.stack(mats).astype(np.float32))


def _delta_merge_mask_array():
    r = np.arange(CHUNK)[:, None]
    c = np.arange(CHUNK)[None, :]
    mats = []
    for b in _MERGE_BLOCKS:
        off = ((r // (2 * b)) == (c // (2 * b))) & ((r // b) != (c // b))
        for rev in (0, 1):
            mats.append(np.concatenate([off[s * b:(s + 1) * b] for s in range(1 - rev, CHUNK // b, 2)], axis=0))
    return jnp.asarray(np.stack(mats).astype(np.float32))


def _gla_mask_array():
    r = np.arange(CHUNK)[:, None]
    c = np.arange(CHUNK)[None, :]
    same = (r // SUB) == (c // SUB)
    mats = [(r >= c) & same, (r // SUB > c // SUB), (r <= c) & same, (r // SUB < c // SUB)]
    return jnp.asarray(np.stack(mats).astype(np.float32))


def _delta_prep_group(qs, ks, vs, gates, gc_rows, revs, heads, m_ref, mm_ref):
    lane = lax.broadcasted_iota(jnp.int32, (CHUNK, LANES), 1)

    def pick(gt, idx):
        return jnp.broadcast_to(jnp.sum(jnp.where(lane == idx, gt, 0.0), axis=1, keepdims=True), (CHUNK, LANES))

    gc = [pick(gt, rev * DN_HEADS + h) for gt, rev, h in zip(gates, revs, heads)]
    beta = [pick(gt, (2 + rev) * DN_HEADS + h) for gt, rev, h in zip(gates, revs, heads)]
    decay = [jnp.exp(jnp.minimum(x - y, 0.0)) for x, y in zip(gc, gc_rows)]
    kq = [_mm_nt(jnp.concatenate([k, q], axis=0), k) for q, k in zip(qs, ks)]
    lm = [x[:CHUNK] * dec * m_ref[2 * rev + 1] * bt for x, dec, rev, bt in zip(kq, decay, revs, beta)]
    a = [(x[CHUNK:] * dec * m_ref[2 * rev]).astype(BF16) for x, dec, rev in zip(kq, decay, revs)]

    lb = [x * m_ref[_M_DIAG] for x in lm]
    l2 = [_mm(x, x) for x in lb]
    l4 = [_mm(x, x) for x in l2]
    inv = [m_ref[_M_EYE] - x for x in lb]
    inv = [x + _mm(x, y) for x, y in zip(inv, l2)]
    inv = [x + _mm(x, y) for x, y in zip(inv, l4)]
    lmb = [x.astype(BF16) for x in lm]
    for n, b in enumerate(_MERGE_BLOCKS):
        own = [_half_rows(x, b, 1 - rev) for x, rev in zip(inv, revs)]
        xc = [jnp.dot(x.astype(BF16), y, preferred_element_type=F32) * mm_ref[2 * n + rev]
              for x, y, rev in zip(own, lmb, revs)]
        own = [x - _mm(y, z) for x, y, z in zip(own, xc, inv)]
        inv = [_put_half_rows(x, y, b, 1 - rev) for x, y, rev in zip(inv, own, revs)]

    kf = [k.astype(F32) for k in ks]
    e_pos = [jnp.exp(x) for x in gc]
    ends = [x[0:1, :] if rev else x[CHUNK - 1:CHUNK, :] for x, rev in zip(gc, revs)]
    rhs = [jnp.concatenate([(k * (bt * e)).astype(BF16),
                            (v.astype(F32) * jnp.concatenate([bt] * (DN_DV // LANES), axis=1)).astype(BF16)], axis=1)
           for k, v, bt, e in zip(kf, vs, beta, e_pos)]
    wu = [jnp.dot(x.astype(BF16), y, preferred_element_type=F32) for x, y in zip(inv, rhs)]
    w = [x[:, :DN_DK].astype(BF16) for x in wu]
    u = [x[:, DN_DK:].astype(BF16) for x in wu]
    kdt = [(k * jnp.exp(end - x)).T.astype(BF16) for k, end, x in zip(kf, ends, gc)]
    aw = [jnp.dot(jnp.concatenate([x, y], axis=0), z, preferred_element_type=F32) for x, y, z in zip(kdt, a, w)]
    lhs = [jnp.concatenate([jnp.concatenate([(-x[:DN_DK]).astype(BF16), y], axis=1),
                            jnp.concatenate([(q.astype(F32) * e - x[DN_DK:]).astype(BF16), z], axis=1)], axis=0)
           for x, y, z, q, e in zip(aw, kdt, a, qs, e_pos)]
    cdec = [jnp.concatenate([jnp.exp(end)] * (DN_DV // LANES), axis=1) for end in ends]
    return lhs, u, cdec


def _delta_scan_kernel(qf_ref, kf_ref, vf_ref, gf_ref, gtf_ref, qb_ref, kb_ref, vb_ref, gb_ref, gtb_ref,
                       s0_ref, m_ref, mm_ref, of_ref, ob_ref, sfin_ref, s_scr, *, nc, par, hpg):
    step = pl.program_id(2)

    @pl.when(step == 0)
    def _():
        s_scr[...] = s0_ref[0]

    fwd_refs = (qf_ref, kf_ref, vf_ref, gf_ref, gtf_ref, of_ref)
    bwd_refs = (qb_ref, kb_ref, vb_ref, gb_ref, gtb_ref, ob_ref)

    def body(it, carry):
        chains = []
        for hh in range(hpg):
            head = pl.program_id(1) * hpg + hh
            for rev in (0, 1):
                for p in range(par):
                    c = nc - 1 - it * par - p if rev else it * par + p
                    chains.append((hh, head, rev, pl.ds(pl.multiple_of(c * CHUNK, CHUNK), CHUNK)))

        def tile(slot, width):
            return [(bwd_refs if rev else fwd_refs)[slot][0, rows, hh * width:(hh + 1) * width]
                    for hh, _, rev, rows in chains]

        gates = [(bwd_refs if rev else fwd_refs)[3][0, rows, :] for _, _, rev, rows in chains]
        gc_rows = [(bwd_refs if rev else fwd_refs)[4][
            0, pl.ds(pl.multiple_of((rev * DN_HEADS + head) * SUBLANES, SUBLANES), SUBLANES), rows][0:1]
            for _, head, rev, rows in chains]
        lhs, u, cdec = _delta_prep_group(tile(0, DN_DK), tile(1, DN_DK), tile(2, DN_DV), gates, gc_rows,
                                         [rev for _, _, rev, _ in chains], [head for _, head, _, _ in chains],
                                         m_ref, mm_ref)
        seqs = [(hh, rev) for hh in range(hpg) for rev in (0, 1)]
        state = [s_scr[hh, rev] for hh, rev in seqs]
        for p in range(par):
            idx = [(hh * 2 + rev) * par + p for hh, rev in seqs]
            res = [jnp.dot(lhs[n], jnp.concatenate([s.astype(BF16), u[n]], axis=0), preferred_element_type=F32)
                   for s, n in zip(state, idx)]
            state = [cdec[n] * s + r[:DN_DK] for s, r, n in zip(state, res, idx)]
            for r, n in zip(res, idx):
                hh, _, rev, rows = chains[n]
                o_ref = (bwd_refs if rev else fwd_refs)[5]
                o_ref[0, rows, hh * DN_DV:(hh + 1) * DN_DV] = r[DN_DK:].astype(o_ref.dtype)
        for (hh, rev), s in zip(seqs, state):
            s_scr[hh, rev] = s
        return carry

    lax.fori_loop(0, nc // par, body, 0)

    @pl.when(step == pl.num_programs(2) - 1)
    def _():
        sfin_ref[0] = s_scr[...]


def _delta_scan(qkv, gates, gates_t, s0, masks, merge_masks):
    bsz, t, _ = qkv.shape
    tb = min(t, 1024)
    ns = t // tb
    hpg = DELTA_HEADS_PER_STEP
    groups = DN_HEADS // hpg

    def fwd(width, off):
        return pl.BlockSpec((1, tb, hpg * width), lambda b, h, s: (b, s, off + h))

    def bwd(width, off):
        return pl.BlockSpec((1, tb, hpg * width), lambda b, h, s: (b, ns - 1 - s, off + h))

    state_spec = pl.BlockSpec((1, hpg, 2, DN_DK, DN_DV), lambda b, h, s: (b, h, 0, 0, 0))
    o_shape = jax.ShapeDtypeStruct((bsz, t, DN_V), BF16)
    return pl.pallas_call(
        functools.partial(_delta_scan_kernel, nc=tb // CHUNK, par=min(SCAN_PAR, tb // CHUNK), hpg=hpg),
        grid=(bsz, groups, ns),
        in_specs=[fwd(DN_DK, 0), fwd(DN_DK, groups), fwd(DN_DV, groups),
                  pl.BlockSpec((1, tb, LANES), lambda b, h, s: (b, s, 0)),
                  pl.BlockSpec((1, LANES, tb), lambda b, h, s: (b, 0, s)),
                  bwd(DN_DK, 0), bwd(DN_DK, groups), bwd(DN_DV, groups),
                  pl.BlockSpec((1, tb, LANES), lambda b, h, s: (b, ns - 1 - s, 0)),
                  pl.BlockSpec((1, LANES, tb), lambda b, h, s: (b, 0, ns - 1 - s)),
                  state_spec,
                  pl.BlockSpec(masks.shape, lambda b, h, s: (0, 0, 0)),
                  pl.BlockSpec(merge_masks.shape, lambda b, h, s: (0, 0, 0))],
        out_specs=[fwd(DN_DV, 0), bwd(DN_DV, 0), state_spec],
        out_shape=[o_shape, o_shape, jax.ShapeDtypeStruct(s0.shape, F32)],
        scratch_shapes=[pltpu.VMEM((hpg, 2, DN_DK, DN_DV), F32)],
        compiler_params=_params(("parallel", "parallel", "arbitrary")),
        name="delta_scan",
    )(qkv, qkv, qkv, gates, gates_t, qkv, qkv, qkv, gates, gates_t, s0, masks, merge_masks)


def _gla_prep_group(qs, ks, lgs, revs, m_ref):
    in_hi = lax.broadcasted_iota(jnp.int32, (CHUNK, GLA_DK), 0) >= SUB
    bc = [_masked_sums(m_ref[2 * rev], lg) for lg, rev in zip(lgs, revs)]

    def per_block(x, rows):
        return jnp.concatenate([jnp.broadcast_to(x[r:r + 1], (SUB, GLA_DK)) for r in rows], axis=0)

    ref_rows = [(SUB // 2 - 1, SUB + SUB // 2 - 1) if rev else (SUB // 2, SUB + SUB // 2) for rev in revs]
    last_rows = [(0, SUB) if rev else (SUB - 1, CHUNK - 1) for rev in revs]
    refb = [per_block(x, rows) for x, rows in zip(bc, ref_rows)]
    lastb = [per_block(x, rows) for x, rows in zip(bc, last_rows)]
    qf = [q.astype(F32) * (GLA_DK ** -0.5) for q in qs]
    kf = [k.astype(F32) for k in ks]
    qe = [q * jnp.exp(x - r) for q, x, r in zip(qf, bc, refb)]
    ke = [k * jnp.exp(r - x) for k, x, r in zip(kf, bc, refb)]
    qd = [q * jnp.exp(x) for q, x in zip(qf, bc)]
    kd = [k * jnp.exp(l - x) for k, x, l in zip(kf, bc, lastb)]
    a_in = [_mm_nt(x, y) for x, y in zip(qe, ke)]
    a_x = [_mm_nt(x, y) for x, y in zip(qd, kd)]
    a = [x * m_ref[2 * rev] + y * m_ref[2 * rev + 1] for x, y, rev in zip(a_in, a_x, revs)]
    last1 = [x[rows[1]:rows[1] + 1] if rev else x[rows[0]:rows[0] + 1] for x, rows, rev in zip(bc, last_rows, revs)]
    last2 = [x[rows[0]:rows[0] + 1] if rev else x[rows[1]:rows[1] + 1] for x, rows, rev in zip(bc, last_rows, revs)]
    second = [jnp.logical_not(in_hi) if rev else in_hi for rev in revs]
    qd2 = [x * jnp.where(sec, jnp.exp(l1), 1.0) for x, sec, l1 in zip(qd, second, last1)]
    kd2 = [x * jnp.where(sec, 1.0, jnp.exp(l2)) for x, sec, l2 in zip(kd, second, last2)]
    cdcol = [jnp.broadcast_to(jnp.exp(l1 + l2), (GLA_DK, LANES)).T for l1, l2 in zip(last1, last2)]
    lhs = [jnp.concatenate([x.astype(BF16), y.astype(BF16)], axis=1) for x, y in zip(qd2, a)]
    kdt = [x.T.astype(BF16) for x in kd2]
    decay = [jnp.concatenate([x] * (GLA_DV // LANES), axis=1) for x in cdcol]
    return lhs, kdt, decay


def _gla_scan_kernel(qf_ref, kf_ref, vf_ref, gf_ref, qb_ref, kb_ref, vb_ref, gb_ref, s0_ref, m_ref,
                     of_ref, ob_ref, sfin_ref, s_scr, *, nc, par):
    step = pl.program_id(2)

    @pl.when(step == 0)
    def _():
        s_scr[...] = s0_ref[0, 0]

    def body(it, carry):
        rows = ([pl.ds(pl.multiple_of((it * par + p) * CHUNK, CHUNK), CHUNK) for p in range(par)]
                + [pl.ds(pl.multiple_of((nc - 1 - it * par - p) * CHUNK, CHUNK), CHUNK) for p in range(par)])
        refs = [(qf_ref, kf_ref, vf_ref, gf_ref, of_ref)] * par + [(qb_ref, kb_ref, vb_ref, gb_ref, ob_ref)] * par
        revs = [0] * par + [1] * par
        lhs, kdt, decay = _gla_prep_group([r[0][0, rw, :] for r, rw in zip(refs, rows)],
                                          [r[1][0, rw, :] for r, rw in zip(refs, rows)],
                                          [r[3][0, rw, :] for r, rw in zip(refs, rows)], revs, m_ref)
        vs = [r[2][0, rw, :] for r, rw in zip(refs, rows)]
        upd = [jnp.dot(x, v, preferred_element_type=F32) for x, v in zip(kdt, vs)]
        state = [s_scr[0], s_scr[1]]
        for p in range(par):
            idx = [p, par + p]
            outs = [jnp.dot(lhs[n], jnp.concatenate([state[d].astype(BF16), vs[n]], axis=0),
                            preferred_element_type=F32) for d, n in enumerate(idx)]
            state = [decay[n] * state[d] + upd[n] for d, n in enumerate(idx)]
            for d, n in enumerate(idx):
                refs[n][4][0, rows[n], :] = outs[d].astype(refs[n][4].dtype)
        s_scr[0] = state[0]
        s_scr[1] = state[1]
        return carry

    lax.fori_loop(0, nc // par, body, 0)

    @pl.when(step == pl.num_programs(2) - 1)
    def _():
        sfin_ref[0, 0] = s_scr[...]


def _gla_scan(p, lg, s0, masks):
    bsz, t, _ = p.shape
    tb = min(t, 1024)
    ns = t // tb
    q_off = GLA_V // GLA_DK
    k_off = q_off + GLA_HEADS
    v_off = (GLA_V + 2 * GLA_K) // GLA_DV

    def spec(width, off, rev):
        if rev:
            return pl.BlockSpec((1, tb, width), lambda b, h, s: (b, ns - 1 - s, off + h))
        return pl.BlockSpec((1, tb, width), lambda b, h, s: (b, s, off + h))

    state_spec = pl.BlockSpec((1, 1, 2, GLA_DK, GLA_DV), lambda b, h, s: (b, h, 0, 0, 0))
    o_shape = jax.ShapeDtypeStruct((bsz, t, GLA_V), BF16)
    return pl.pallas_call(
        functools.partial(_gla_scan_kernel, nc=tb // CHUNK, par=min(GLA_SCAN_PAR, tb // CHUNK)),
        grid=(bsz, GLA_HEADS, ns),
        in_specs=[spec(GLA_DK, q_off, 0), spec(GLA_DK, k_off, 0), spec(GLA_DV, v_off, 0), spec(GLA_DK, 0, 0),
                  spec(GLA_DK, q_off, 1), spec(GLA_DK, k_off, 1), spec(GLA_DV, v_off, 1),
                  spec(GLA_DK, GLA_HEADS, 1),
                  state_spec,
                  pl.BlockSpec(masks.shape, lambda b, h, s: (0, 0, 0))],
        out_specs=[spec(GLA_DV, 0, 0), spec(GLA_DV, 0, 1), state_spec],
        out_shape=[o_shape, o_shape, jax.ShapeDtypeStruct(s0.shape, F32)],
        scratch_shapes=[pltpu.VMEM((2, GLA_DK, GLA_DV), F32)],
        compiler_params=_params(("parallel", "parallel", "arbitrary")),
        name="gla_scan",
    )(p, p, p, lg, p, p, p, lg, s0, masks)


def _gated_out_kernel(of_ref, ob_ref, z_ref, x_ref, gt_ref, ng_ref, w_ref, fg_ref, o_ref, *, dv, final):
    o = of_ref[0].astype(F32) + ob_ref[0].astype(F32)
    parts = []
    for s in range(o.shape[1] // dv):
        seg = o[:, s * dv:(s + 1) * dv]
        ms = jnp.mean(seg * seg, axis=-1, keepdims=True)
        parts.append(seg * lax.rsqrt(ms + EPS))
    y = jnp.concatenate(parts, axis=1) * ng_ref[...] * _silu(z_ref[0].astype(F32))
    proj = jnp.dot(y.astype(BF16), w_ref[...], preferred_element_type=F32)
    xn = x_ref[0] + gt_ref[0] * proj
    if final:
        ms = jnp.mean(xn * xn, axis=-1, keepdims=True)
        xn = xn * lax.rsqrt(ms + EPS) * fg_ref[...]
    o_ref[0] = xn


def _gated_out(o_f, o_b, p, z_block, x, gt, norm_g_row, w_out, final_g_row, dv, final):
    bsz, t, d = x.shape
    dvt = o_f.shape[2]
    tm = min(t, 512)
    const = lambda b, i: (0, 0)
    return pl.pallas_call(
        functools.partial(_gated_out_kernel, dv=dv, final=final),
        grid=(bsz, t // tm),
        in_specs=[pl.BlockSpec((1, tm, dvt), lambda b, i: (b, i, 0)),
                  pl.BlockSpec((1, tm, dvt), lambda b, i: (b, i, 0)),
                  pl.BlockSpec((1, tm, dvt), lambda b, i: (b, i, z_block)),
                  pl.BlockSpec((1, tm, d), lambda b, i: (b, i, 0)),
                  pl.BlockSpec((1, 1, d), lambda b, i: (b, 0, 0)),
                  pl.BlockSpec((1, dvt), const),
                  pl.BlockSpec((dvt, d), const),
                  pl.BlockSpec((1, d), const)],
        out_specs=pl.BlockSpec((1, tm, d), lambda b, i: (b, i, 0)),
        out_shape=jax.ShapeDtypeStruct((bsz, t, d), F32),
        compiler_params=_params(("parallel", "parallel")),
        name="gated_out_final" if final else "gated_out",
    )(o_f, o_b, p, x, gt, norm_g_row, w_out, final_g_row)


def _pad_cols(w, width):
    return jnp.pad(w, ((0, 0), (0, width - w.shape[1])))


def kernel(x, c, ctx, c_ctx, mod_w, mod_b, norm_g, dn_w_in, dn_conv_w, dn_a_log, dn_dt_bias, dn_norm_g,
           dn_w_out, gla_w_in, gla_w_g2, gla_b_g, gla_norm_g, gla_w_out, final_g):
    bsz, _, d = x.shape
    assert bsz + 1 <= 8
    cond = jnp.concatenate([c, c_ctx[None], jnp.zeros((8 - bsz - 1, d), F32)], axis=0)
    mod = _modulation(cond, mod_w, mod_b)

    def mod_rows(layer):
        lat = [mod[layer, :bsz, None, n * d:(n + 1) * d] for n in range(3)]
        cx = [jnp.broadcast_to(mod[layer, bsz, n * d:(n + 1) * d], (bsz, 1, d)) for n in range(3)]
        return lat, cx

    final_row = final_g.reshape(1, d)

    (sh_l, sc_l, gt_l), (sh_c, sc_c, gt_c) = mod_rows(0)
    w0 = dn_w_in[0].astype(BF16)
    w0_main, w0_small = w0[:, :DN_MAIN], _pad_cols(w0[:, DN_MAIN:], LANES)
    gate_params = jnp.zeros((8, LANES), F32)
    gate_params = gate_params.at[0, :2 * DN_HEADS].set(dn_a_log[0].reshape(-1))
    gate_params = gate_params.at[1, :2 * DN_HEADS].set(dn_dt_bias[0].reshape(-1))
    g0 = norm_g[0].reshape(1, d)
    conv_w9 = dn_conv_w[0].reshape(9, DN_QKV)
    dmasks = (_delta_mask_array(), _delta_merge_mask_array())
    dn_g_row = jnp.tile(dn_norm_g[0], DN_HEADS).reshape(1, DN_V)
    dn_wo = dn_w_out[0].astype(BF16)

    p_c, gates_c, gates_tc = _inproj(_dn_inproj_kernel, "dn_inproj", ctx, sh_c, sc_c, g0, w0_main, w0_small,
                                     [gate_params], LANES, aux_transposed=True)
    qkv_c = _conv_seq(p_c, conv_w9)
    qkv_l, z_l, gates_l, gates_tl = _dn_inproj_conv(x, sh_l, sc_l, g0, w0_main, w0_small, gate_params, conv_w9)
    s_zero = jnp.zeros((bsz, DN_HEADS, 2, DN_DK, DN_DV), F32)
    oc_f, oc_b, s_ctx = _delta_scan(qkv_c, gates_c, gates_tc, s_zero, *dmasks)
    ol_f, ol_b, _ = _delta_scan(qkv_l, gates_l, gates_tl, s_ctx, *dmasks)
    ctx1 = _gated_out(oc_f, oc_b, p_c, DN_QKV // DN_V, ctx, gt_c, dn_g_row, dn_wo, final_row, DN_DV, False)
    x1 = _gated_out(ol_f, ol_b, z_l, 0, x, gt_l, dn_g_row, dn_wo, final_row, DN_DV, False)

    (sh_l, sc_l, gt_l), (sh_c, sc_c, _) = mod_rows(1)
    w1 = gla_w_in[0].astype(BF16)
    w1_main = jnp.concatenate([w1[:, 2 * GLA_K + GLA_V:GLA_MAIN], w1[:, :2 * GLA_K + GLA_V]], axis=1)
    w1_small = _pad_cols(w1[:, GLA_MAIN:], LANES)
    wg = jnp.zeros((LANES, 2 * GLA_K), F32)
    wg = wg.at[:GLA_RANK, :GLA_K].set(gla_w_g2[0, 0]).at[GLA_RANK:2 * GLA_RANK, GLA_K:].set(gla_w_g2[0, 1])
    wg = wg.astype(BF16)
    bg = gla_b_g[0].reshape(1, 2 * GLA_K)
    g1 = norm_g[1].reshape(1, d)
    gmasks = _gla_mask_array()
    gla_g_row = jnp.tile(gla_norm_g[0], GLA_HEADS).reshape(1, GLA_V)
    gla_wo = gla_w_out[0].astype(BF16)

    def gla_features(u, sh, sc):
        return _gla_inproj(u, sh, sc, g1, w1_main, w1_small, wg, bg)

    pg_c, lg_c = gla_features(ctx1, sh_c, sc_c)
    pg_l, lg_l = gla_features(x1, sh_l, sc_l)
    s_zero = jnp.zeros((bsz, GLA_HEADS, 2, GLA_DK, GLA_DV), F32)
    _, _, s_ctx = _gla_scan(pg_c, lg_c, s_zero, gmasks)
    og_f, og_b, _ = _gla_scan(pg_l, lg_l, s_ctx, gmasks)
    return _gated_out(og_f, og_b, pg_l, 0, x1, gt_l, gla_g_row, gla_wo, final_row, GLA_DV, True)
```

```python
import functools

import numpy as np
import jax
import jax.numpy as jnp
from jax import lax
from jax.experimental import pallas as pl
from jax.experimental.pallas import tpu as pltpu

F32, BF16 = jnp.float32, jnp.bfloat16
HIGHEST = lax.Precision.HIGHEST
EPS = 1e-6

LANES = 128
SUBLANES = 8
GRID_W = 64
CHUNK = 128
SUB = 64
DELTA_SCAN_BLOCK = 1024
GLA_SCAN_BLOCK = 2048
SCAN_PAR = 2
GLA_SCAN_PAR = 4
DELTA_HEADS_PER_STEP = 4
VMEM_LIMIT_BYTES = 52 * 1024 * 1024
FUSED_VMEM_LIMIT_BYTES = 58 * 1024 * 1024

DN_DK, DN_HEADS, DN_DV = 128, 8, 256
CONV_K = 3
DN_QK = DN_HEADS * DN_DK
DN_V = DN_HEADS * DN_DV
DN_QKV = 2 * DN_QK + DN_V
DN_MAIN = DN_QKV + DN_V
GLA_HEADS, GLA_DK, GLA_DV, GLA_RANK, GLA_TAU = 4, 128, 512, 16, 16.0
GLA_K = GLA_HEADS * GLA_DK
GLA_V = GLA_HEADS * GLA_DV
GLA_MAIN = 2 * GLA_K + 2 * GLA_V


def _sigmoid(x):
    return 1.0 / (1.0 + jnp.exp(-x))


def _silu(x):
    return x * _sigmoid(x)


def _softplus(x):
    return jnp.maximum(x, 0.0) + jnp.log(1.0 + jnp.exp(-jnp.abs(x)))


def _mm(a, b):
    return jnp.dot(a.astype(BF16), b.astype(BF16), preferred_element_type=F32)


def _mm_nt(a, b):
    return lax.dot_general(a.astype(BF16), b.astype(BF16), (((1,), (1,)), ((), ())),
                           preferred_element_type=F32)


def _masked_sums(mask01, x):
    hi = x.astype(BF16)
    r1 = x - hi.astype(F32)
    mid = r1.astype(BF16)
    lo = (r1 - mid.astype(F32)).astype(BF16)
    n = x.shape[1]
    s = jnp.dot(mask01.astype(BF16), jnp.concatenate([hi, mid, lo], axis=1), preferred_element_type=F32)
    return (s[:, :n] + s[:, n:2 * n]) + s[:, 2 * n:]


def _mm_f32(a, b):
    return jnp.dot(a, b, precision=HIGHEST, preferred_element_type=F32)


def _params(semantics):
    return pltpu.CompilerParams(dimension_semantics=semantics, vmem_limit_bytes=VMEM_LIMIT_BYTES)


def _mod_kernel(c_ref, w_ref, b_ref, o_ref):
    o_ref[0] = _mm_f32(_silu(c_ref[...]), w_ref[0]) + b_ref[0]


def _modulation(cond, mod_w, mod_b):
    depth, d, d3 = mod_w.shape
    tn = d
    return pl.pallas_call(
        _mod_kernel,
        grid=(depth, d3 // tn),
        in_specs=[pl.BlockSpec((8, d), lambda l, j: (0, 0)),
                  pl.BlockSpec((1, d, tn), lambda l, j: (l, 0, j)),
                  pl.BlockSpec((1, 1, tn), lambda l, j: (l, 0, j))],
        out_specs=pl.BlockSpec((1, 8, tn), lambda l, j: (l, 0, j)),
        out_shape=jax.ShapeDtypeStruct((depth, 8, d3), F32),
        compiler_params=_params(("parallel", "parallel")),
        name="modulation",
    )(cond, mod_w, mod_b.reshape(depth, 1, d3))


NORM_STRIP = 32


def _norm_mod_to(h_scr, x_ref, g_ref, sc_ref, sh_ref, dst_row=0, valid=None):
    gain = g_ref[...] * (1.0 + sc_ref[0])
    shift = sh_ref[0]
    n_strips = x_ref.shape[1] // NORM_STRIP

    def strip(n, carry):
        x = x_ref[0, pl.ds(pl.multiple_of(n * NORM_STRIP, NORM_STRIP), NORM_STRIP), :]
        ms = jnp.mean(x * x, axis=-1, keepdims=True)
        h = (x * lax.rsqrt(ms + EPS)) * gain + shift
        if valid is not None:
            h = jnp.where(valid, h, 0.0)
        h_scr[pl.ds(pl.multiple_of(dst_row + n * NORM_STRIP, NORM_STRIP), NORM_STRIP), :] = h.astype(BF16)
        return carry

    lax.fori_loop(0, n_strips, strip, 0, unroll=min(8, n_strips))


def _dn_gates(hb, ws_ref, ap_ref, gates_ref, gates_t_ref):
    raw = jnp.dot(hb, ws_ref[...], preferred_element_type=F32)
    log_decay = -jnp.exp(ap_ref[0:1, :]) * _softplus(raw + ap_ref[1:2, :])
    beta = _sigmoid(raw)
    r = lax.broadcasted_iota(jnp.int32, (2 * CHUNK, CHUNK), 0)
    c = lax.broadcasted_iota(jnp.int32, (2 * CHUNK, CHUNK), 1)
    tri2 = (jnp.where(r < CHUNK, r - c, c - r + CHUNK) >= 0).astype(F32)
    lane = lax.broadcasted_iota(jnp.int32, (CHUNK, LANES), 1)
    for n in range(raw.shape[0] // CHUNK):
        rows = slice(n * CHUNK, (n + 1) * CHUNK)
        cs = _masked_sums(tri2, log_decay[rows])
        gt = jnp.where(lane < DN_HEADS, cs[:CHUNK], jnp.where(lane < 2 * DN_HEADS, cs[CHUNK:], beta[rows]))
        gates_ref[0, rows, :] = gt
        gtt = gt.T
        gates_t_ref[0, :, rows] = jnp.concatenate(
            [jnp.broadcast_to(gtt[m:m + 1], (SUBLANES, CHUNK)) for m in range(2 * DN_HEADS)], axis=0)


def _dn_inproj_kernel(x_ref, sh_ref, sc_ref, g_ref, w_ref, ws_ref, ap_ref, o_ref, gates_ref, gates_t_ref, h_scr):
    @pl.when(pl.program_id(2) == 0)
    def _():
        _norm_mod_to(h_scr, x_ref, g_ref, sc_ref, sh_ref)
        _dn_gates(h_scr[...], ws_ref, ap_ref, gates_ref, gates_t_ref)

    o_ref[0] = jnp.dot(h_scr[...], w_ref[...], preferred_element_type=F32).astype(o_ref.dtype)


def _conv_strip(up, mid, down, w, first, last, scale):
    accs = [w[dc] * up + w[3 + dc] * mid + w[6 + dc] * down for dc in range(3)]
    y = _silu(accs[1] + jnp.where(first, 0.0, pltpu.roll(accs[0], 1, 0))
              + jnp.where(last, 0.0, pltpu.roll(accs[2], GRID_W - 1, 0)))
    if scale is not None:
        y = y * (lax.rsqrt(jnp.sum(y * y, axis=-1, keepdims=True) + EPS) * scale)
    return y


def _dn_inproj_conv_kernel(xp_ref, x_ref, xn_ref, sh_ref, sc_ref, g_ref, w_ref, ws_ref, ap_ref, cw_ref,
                           qkv_ref, z_ref, gates_ref, gates_t_ref, h_scr, p_scr, *, tw):
    i = pl.program_id(1)
    tm = x_ref.shape[1]
    n_rows = tm // GRID_W
    _norm_mod_to(h_scr, xp_ref, g_ref, sc_ref, sh_ref, 0, i > 0)
    _norm_mod_to(h_scr, x_ref, g_ref, sc_ref, sh_ref, GRID_W)
    _norm_mod_to(h_scr, xn_ref, g_ref, sc_ref, sh_ref, GRID_W + tm, i < pl.num_programs(1) - 1)
    _dn_gates(h_scr[GRID_W:GRID_W + tm, :], ws_ref, ap_ref, gates_ref, gates_t_ref)

    pos = lax.broadcasted_iota(jnp.int32, (GRID_W, DN_DK), 0)
    first, last = pos == 0, pos == GRID_W - 1
    n_conv = DN_QKV // tw
    n_z = DN_V // tw

    def conv_tile(t):
        cols0 = t * tw
        scale = DN_DK ** -0.5 if cols0 < DN_QK else (1.0 if cols0 < 2 * DN_QK else None)
        for s in range(tw // DN_DK):
            lanes = slice(s * DN_DK, (s + 1) * DN_DK)
            w = [cw_ref[k:k + 1, cols0 + s * DN_DK:cols0 + (s + 1) * DN_DK] for k in range(9)]
            up = p_scr[t % 2, 0:GRID_W, lanes]
            mid = p_scr[t % 2, GRID_W:2 * GRID_W, lanes]
            for r in range(n_rows):
                down = p_scr[t % 2, (r + 2) * GRID_W:(r + 3) * GRID_W, lanes]
                y = _conv_strip(up, mid, down, w, first, last, scale)
                qkv_ref[0, r * GRID_W:(r + 1) * GRID_W, cols0 + s * DN_DK:cols0 + (s + 1) * DN_DK] = (
                    y.astype(qkv_ref.dtype))
                up, mid = mid, down

    for t in range(n_conv):
        p_scr[t % 2] = jnp.dot(h_scr[...], w_ref[:, t * tw:(t + 1) * tw], preferred_element_type=F32)
        if t > 0:
            conv_tile(t - 1)
    for t in range(n_z):
        z_ref[0, :, t * tw:(t + 1) * tw] = jnp.dot(
            h_scr[GRID_W:GRID_W + tm, :], w_ref[:, DN_QKV + t * tw:DN_QKV + (t + 1) * tw],
            preferred_element_type=F32).astype(z_ref.dtype)
        if t == 0:
            conv_tile(n_conv - 1)


def _dn_inproj_conv(x, sh, sc, g, w_main, w_small, gate_params, conv_w9):
    bsz, t, d = x.shape
    tm = 512
    tw = 1024
    rows_per_tile = tm // GRID_W
    n_rows = t // GRID_W
    const = lambda b, i: (0, 0)
    return pl.pallas_call(
        functools.partial(_dn_inproj_conv_kernel, tw=tw),
        grid=(bsz, t // tm),
        in_specs=[pl.BlockSpec((1, GRID_W, d), lambda b, i: (b, jnp.maximum(i * rows_per_tile - 1, 0), 0)),
                  pl.BlockSpec((1, tm, d), lambda b, i: (b, i, 0)),
                  pl.BlockSpec((1, GRID_W, d), lambda b, i: (b, jnp.minimum((i + 1) * rows_per_tile, n_rows - 1), 0)),
                  pl.BlockSpec((1, 1, d), lambda b, i: (b, 0, 0)),
                  pl.BlockSpec((1, 1, d), lambda b, i: (b, 0, 0)),
                  pl.BlockSpec((1, d), const),
                  pl.BlockSpec(w_main.shape, const),
                  pl.BlockSpec((d, LANES), const),
                  pl.BlockSpec(gate_params.shape, const),
                  pl.BlockSpec(conv_w9.shape, const)],
        out_specs=[pl.BlockSpec((1, tm, DN_QKV), lambda b, i: (b, i, 0)),
                   pl.BlockSpec((1, tm, DN_V), lambda b, i: (b, i, 0)),
                   pl.BlockSpec((1, tm, LANES), lambda b, i: (b, i, 0)),
                   pl.BlockSpec((1, LANES, tm), lambda b, i: (b, 0, i))],
        out_shape=[jax.ShapeDtypeStruct((bsz, t, DN_QKV), BF16), jax.ShapeDtypeStruct((bsz, t, DN_V), BF16),
                   jax.ShapeDtypeStruct((bsz, t, LANES), F32), jax.ShapeDtypeStruct((bsz, LANES, t), F32)],
        scratch_shapes=[pltpu.VMEM((tm + 2 * GRID_W, d), BF16), pltpu.VMEM((2, tm + 2 * GRID_W, tw), F32)],
        compiler_params=pltpu.CompilerParams(dimension_semantics=("parallel", "parallel"),
                                             vmem_limit_bytes=FUSED_VMEM_LIMIT_BYTES),
        name="dn_inproj_conv",
    )(x, x, x, sh, sc, g, w_main, w_small, gate_params, conv_w9)


def _gla_inproj_kernel(x_ref, sh_ref, sc_ref, g_ref, w_ref, ws_ref, wg_ref, bg_ref, o_ref, lg_ref, h_scr, *, tw):
    _norm_mod_to(h_scr, x_ref, g_ref, sc_ref, sh_ref)
    hb = h_scr[...]
    glb = jnp.dot(hb, ws_ref[...], preferred_element_type=F32).astype(BF16)
    n_tiles = w_ref.shape[1] // tw
    for t in range(n_tiles):
        o_ref[0, :, t * tw:(t + 1) * tw] = jnp.dot(hb, w_ref[:, t * tw:(t + 1) * tw],
                                                   preferred_element_type=F32).astype(o_ref.dtype)
        for r0 in range(t * CHUNK, glb.shape[0], n_tiles * CHUNK):
            zg = jnp.dot(glb[r0:r0 + CHUNK], wg_ref[...], preferred_element_type=F32) + bg_ref[...]
            lg_ref[0, r0:r0 + CHUNK, :] = -_softplus(-zg) * (1.0 / GLA_TAU)


def _gla_inproj(x, sh, sc, g, w_main, w_small, wg, bg):
    bsz, t, d = x.shape
    n = w_main.shape[1]
    tm = min(t, 512)
    const = lambda b, i: (0, 0)
    return pl.pallas_call(
        functools.partial(_gla_inproj_kernel, tw=1024),
        grid=(bsz, t // tm),
        in_specs=[pl.BlockSpec((1, tm, d), lambda b, i: (b, i, 0)),
                  pl.BlockSpec((1, 1, d), lambda b, i: (b, 0, 0)),
                  pl.BlockSpec((1, 1, d), lambda b, i: (b, 0, 0)),
                  pl.BlockSpec((1, d), const),
                  pl.BlockSpec(w_main.shape, const),
                  pl.BlockSpec((d, LANES), const),
                  pl.BlockSpec(wg.shape, const),
                  pl.BlockSpec(bg.shape, const)],
        out_specs=[pl.BlockSpec((1, tm, n), lambda b, i: (b, i, 0)),
                   pl.BlockSpec((1, tm, 2 * GLA_K), lambda b, i: (b, i, 0))],
        out_shape=[jax.ShapeDtypeStruct((bsz, t, n), BF16), jax.ShapeDtypeStruct((bsz, t, 2 * GLA_K), F32)],
        scratch_shapes=[pltpu.VMEM((tm, d), BF16)],
        compiler_params=_params(("parallel", "parallel")),
        name="gla_inproj",
    )(x, sh, sc, g, w_main, w_small, wg, bg)


def _inproj(kernel_fn, name, x, sh, sc, g, w_main, w_small, extra, aux_width, aux_transposed=False):
    bsz, t, d = x.shape
    n = w_main.shape[1]
    tm = min(t, 1024)
    tn = 1024
    const = lambda b, i, j: (0, 0)
    extra_specs = [pl.BlockSpec(e.shape, const) for e in extra]
    out_specs = [pl.BlockSpec((1, tm, tn), lambda b, i, j: (b, i, j)),
                 pl.BlockSpec((1, tm, aux_width), lambda b, i, j: (b, i, 0))]
    out_shape = [jax.ShapeDtypeStruct((bsz, t, n), BF16), jax.ShapeDtypeStruct((bsz, t, aux_width), F32)]
    if aux_transposed:
        out_specs.append(pl.BlockSpec((1, aux_width, tm), lambda b, i, j: (b, 0, i)))
        out_shape.append(jax.ShapeDtypeStruct((bsz, aux_width, t), F32))
    return pl.pallas_call(
        kernel_fn,
        grid=(bsz, t // tm, n // tn),
        in_specs=[pl.BlockSpec((1, tm, d), lambda b, i, j: (b, i, 0)),
                  pl.BlockSpec((1, 1, d), lambda b, i, j: (b, 0, 0)),
                  pl.BlockSpec((1, 1, d), lambda b, i, j: (b, 0, 0)),
                  pl.BlockSpec((1, d), const),
                  pl.BlockSpec((d, tn), lambda b, i, j: (0, j)),
                  pl.BlockSpec((d, LANES), const)] + extra_specs,
        out_specs=out_specs,
        out_shape=out_shape,
        scratch_shapes=[pltpu.VMEM((tm, d), BF16)],
        compiler_params=_params(("parallel", "parallel", "arbitrary")),
        name=name,
    )(x, sh, sc, g, w_main, w_small, *extra)


def _conv_seq_kernel(p_ref, w_ref, o_ref, *, n_q_tiles, n_qk_tiles):
    j = pl.program_id(1)
    _, t, tc = p_ref.shape
    pos = lax.broadcasted_iota(jnp.int32, (t, DN_DK), 0)
    first, last = pos == 0, pos == t - 1

    def run(normalise):
        scale = jnp.where(j < n_q_tiles, DN_DK ** -0.5, 1.0)
        for s in range(tc // DN_DK):
            lanes = slice(s * DN_DK, (s + 1) * DN_DK)
            x = p_ref[0, :, lanes].astype(F32)
            w = [w_ref[CONV_K + dc:CONV_K + dc + 1, lanes] for dc in range(CONV_K)]
            y = _silu(w[1] * x + jnp.where(first, 0.0, pltpu.roll(w[0] * x, 1, 0))
                      + jnp.where(last, 0.0, pltpu.roll(w[2] * x, t - 1, 0)))
            if normalise:
                y = y * (lax.rsqrt(jnp.sum(y * y, axis=-1, keepdims=True) + EPS) * scale)
            o_ref[0, :, lanes] = y.astype(o_ref.dtype)

    @pl.when(j < n_qk_tiles)
    def _():
        run(True)

    @pl.when(j >= n_qk_tiles)
    def _():
        run(False)


def _conv_seq(p, conv_w9):
    bsz, t, _ = p.shape
    tc = 512
    kern = functools.partial(_conv_seq_kernel, n_q_tiles=DN_QK // tc, n_qk_tiles=2 * DN_QK // tc)
    return pl.pallas_call(
        kern,
        grid=(bsz, DN_QKV // tc),
        in_specs=[pl.BlockSpec((1, t, tc), lambda b, j: (b, 0, j)),
                  pl.BlockSpec((CONV_K * CONV_K, tc), lambda b, j: (0, j))],
        out_specs=pl.BlockSpec((1, t, tc), lambda b, j: (b, 0, j)),
        out_shape=jax.ShapeDtypeStruct((bsz, t, DN_QKV), BF16),
        compiler_params=_params(("parallel", "parallel")),
        name="conv_seq",
    )(p, conv_w9)


_BASE_BLOCK = 8
_M_EYE, _M_DIAG = 4, 5
_MERGE_BLOCKS = tuple(_BASE_BLOCK << n for n in range((CHUNK // _BASE_BLOCK).bit_length() - 1))


def _half_rows(x, b, half):
    return jnp.concatenate([x[s * b:(s + 1) * b] for s in range(half, x.shape[0] // b, 2)], axis=0)


def _put_half_rows(x, upd, b, half):
    return jnp.concatenate([upd[(s // 2) * b:(s // 2 + 1) * b] if s % 2 == half else x[s * b:(s + 1) * b]
                            for s in range(x.shape[0] // b)], axis=0)


def _delta_mask_array():
    r = np.arange(CHUNK)[:, None]
    c = np.arange(CHUNK)[None, :]
    mats = [r >= c, r > c, r <= c, r < c, r == c, (r // _BASE_BLOCK) == (c // _BASE_BLOCK)]
    return jnp.asarray(np.stack(mats).astype(np.float32))


def _delta_merge_mask_array():
    r = np.arange(CHUNK)[:, None]
    c = np.arange(CHUNK)[None, :]
    mats = []
    for b in _MERGE_BLOCKS:
        off = ((r // (2 * b)) == (c // (2 * b))) & ((r // b) != (c // b))
        for rev in (0, 1):
            mats.append(np.concatenate([off[s * b:(s + 1) * b] for s in range(1 - rev, CHUNK // b, 2)], axis=0))
    return jnp.asarray(np.stack(mats).astype(np.float32))


def _gla_mask_array():
    r = np.arange(CHUNK)[:, None]
    c = np.arange(CHUNK)[None, :]
    same = (r // SUB) == (c // SUB)
    mats = [(r >= c) & same, (r // SUB > c // SUB), (r <= c) & same, (r // SUB < c // SUB)]
    return jnp.asarray(np.stack(mats).astype(np.float32))


def _delta_prep_group(qs, ks, vs, gates, gc_rows, revs, heads, m_ref, mm_ref):
    lane = lax.broadcasted_iota(jnp.int32, (CHUNK, LANES), 1)

    def pick(gt, idx):
        return jnp.broadcast_to(jnp.sum(jnp.where(lane == idx, gt, 0.0), axis=1, keepdims=True), (CHUNK, LANES))

    gc = [pick(gt, rev * DN_HEADS + h) for gt, rev, h in zip(gates, revs, heads)]
    beta = [pick(gt, (2 + rev) * DN_HEADS + h) for gt, rev, h in zip(gates, revs, heads)]
    decay = [jnp.exp(jnp.minimum(x - y, 0.0)) for x, y in zip(gc, gc_rows)]
    kq = [_mm_nt(jnp.concatenate([k, q], axis=0), k) for q, k in zip(qs, ks)]
    lm = [x[:CHUNK] * dec * m_ref[2 * rev + 1] * bt for x, dec, rev, bt in zip(kq, decay, revs, beta)]
    a = [(x[CHUNK:] * dec * m_ref[2 * rev]).astype(BF16) for x, dec, rev in zip(kq, decay, revs)]

    lb = [x * m_ref[_M_DIAG] for x in lm]
    l2 = [_mm(x, x) for x in lb]
    l4 = [_mm(x, x) for x in l2]
    inv = [m_ref[_M_EYE] - x for x in lb]
    inv = [x + _mm(x, y) for x, y in zip(inv, l2)]
    inv = [x + _mm(x, y) for x, y in zip(inv, l4)]
    lmb = [x.astype(BF16) for x in lm]
    for n, b in enumerate(_MERGE_BLOCKS):
        own = [_half_rows(x, b, 1 - rev) for x, rev in zip(inv, revs)]
        xc = [jnp.dot(x.astype(BF16), y, preferred_element_type=F32) * mm_ref[2 * n + rev]
              for x, y, rev in zip(own, lmb, revs)]
        own = [x - _mm(y, z) for x, y, z in zip(own, xc, inv)]
        inv = [_put_half_rows(x, y, b, 1 - rev) for x, y, rev in zip(inv, own, revs)]

    kf = [k.astype(F32) for k in ks]
    e_pos = [jnp.exp(x) for x in gc]
    ends = [x[0:1, :] if rev else x[CHUNK - 1:CHUNK, :] for x, rev in zip(gc, revs)]
    rhs = [jnp.concatenate([(k * (bt * e)).astype(BF16),
                            (v.astype(F32) * jnp.concatenate([bt] * (DN_DV // LANES), axis=1)).astype(BF16)], axis=1)
           for k, v, bt, e in zip(kf, vs, beta, e_pos)]
    wu = [jnp.dot(x.astype(BF16), y, preferred_element_type=F32) for x, y in zip(inv, rhs)]
    w = [x[:, :DN_DK].astype(BF16) for x in wu]
    u = [x[:, DN_DK:].astype(BF16) for x in wu]
    kdt = [(k * jnp.exp(end - x)).T.astype(BF16) for k, end, x in zip(kf, ends, gc)]
    aw = [jnp.dot(jnp.concatenate([x, y], axis=0), z, preferred_element_type=F32) for x, y, z in zip(kdt, a, w)]
    lhs = [jnp.concatenate([jnp.concatenate([(-x[:DN_DK]).astype(BF16), y], axis=1),
                            jnp.concatenate([(q.astype(F32) * e - x[DN_DK:]).astype(BF16), z], axis=1)], axis=0)
           for x, y, z, q, e in zip(aw, kdt, a, qs, e_pos)]
    cdec = [jnp.concatenate([jnp.exp(end)] * (DN_DV // LANES), axis=1) for end in ends]
    return lhs, u, cdec


def _delta_scan_kernel(qf_ref, kf_ref, vf_ref, gf_ref, gtf_ref, qb_ref, kb_ref, vb_ref, gb_ref, gtb_ref,
                       s0_ref, m_ref, mm_ref, of_ref, ob_ref, sfin_ref, s_scr, *, nc, par, hpg):
    step = pl.program_id(2)

    @pl.when(step == 0)
    def _():
        s_scr[...] = s0_ref[0]

    fwd_refs = (qf_ref, kf_ref, vf_ref, gf_ref, gtf_ref, of_ref)
    bwd_refs = (qb_ref, kb_ref, vb_ref, gb_ref, gtb_ref, ob_ref)

    def body(it, carry):
        chains = []
        for hh in range(hpg):
            head = pl.program_id(1) * hpg + hh
            for rev in (0, 1):
                for p in range(par):
                    c = nc - 1 - it * par - p if rev else it * par + p
                    chains.append((hh, head, rev, pl.ds(pl.multiple_of(c * CHUNK, CHUNK), CHUNK)))

        def tile(slot, width):
            return [(bwd_refs if rev else fwd_refs)[slot][0, rows, hh * width:(hh + 1) * width]
                    for hh, _, rev, rows in chains]

        gates = [(bwd_refs if rev else fwd_refs)[3][0, rows, :] for _, _, rev, rows in chains]
        gc_rows = [(bwd_refs if rev else fwd_refs)[4][
            0, pl.ds(pl.multiple_of((rev * DN_HEADS + head) * SUBLANES, SUBLANES), SUBLANES), rows][0:1]
            for _, head, rev, rows in chains]
        lhs, u, cdec = _delta_prep_group(tile(0, DN_DK), tile(1, DN_DK), tile(2, DN_DV), gates, gc_rows,
                                         [rev for _, _, rev, _ in chains], [head for _, head, _, _ in chains],
                                         m_ref, mm_ref)
        seqs = [(hh, rev) for hh in range(hpg) for rev in (0, 1)]
        state = [s_scr[hh, rev] for hh, rev in seqs]
        for p in range(par):
            idx = [(hh * 2 + rev) * par + p for hh, rev in seqs]
            res = [jnp.dot(lhs[n], jnp.concatenate([s.astype(BF16), u[n]], axis=0), preferred_element_type=F32)
                   for s, n in zip(state, idx)]
            state = [cdec[n] * s + r[:DN_DK] for s, r, n in zip(state, res, idx)]
            for r, n in zip(res, idx):
                hh, _, rev, rows = chains[n]
                o_ref = (bwd_refs if rev else fwd_refs)[5]
                o_ref[0, rows, hh * DN_DV:(hh + 1) * DN_DV] = r[DN_DK:].astype(o_ref.dtype)
        for (hh, rev), s in zip(seqs, state):
            s_scr[hh, rev] = s
        return carry

    lax.fori_loop(0, nc // par, body, 0)

    @pl.when(step == pl.num_programs(2) - 1)
    def _():
        sfin_ref[0] = s_scr[...]


def _delta_scan(qkv, gates, gates_t, s0, masks, merge_masks):
    bsz, t, _ = qkv.shape
    tb = min(t, DELTA_SCAN_BLOCK)
    ns = t // tb
    hpg = DELTA_HEADS_PER_STEP
    groups = DN_HEADS // hpg

    def fwd(width, off):
        return pl.BlockSpec((1, tb, hpg * width), lambda b, h, s: (b, s, off + h))

    def bwd(width, off):
        return pl.BlockSpec((1, tb, hpg * width), lambda b, h, s: (b, ns - 1 - s, off + h))

    state_spec = pl.BlockSpec((1, hpg, 2, DN_DK, DN_DV), lambda b, h, s: (b, h, 0, 0, 0))
    o_shape = jax.ShapeDtypeStruct((bsz, t, DN_V), BF16)
    return pl.pallas_call(
        functools.partial(_delta_scan_kernel, nc=tb // CHUNK, par=min(SCAN_PAR, tb // CHUNK), hpg=hpg),
        grid=(bsz, groups, ns),
        in_specs=[fwd(DN_DK, 0), fwd(DN_DK, groups), fwd(DN_DV, groups),
                  pl.BlockSpec((1, tb, LANES), lambda b, h, s: (b, s, 0)),
                  pl.BlockSpec((1, LANES, tb), lambda b, h, s: (b, 0, s)),
                  bwd(DN_DK, 0), bwd(DN_DK, groups), bwd(DN_DV, groups),
                  pl.BlockSpec((1, tb, LANES), lambda b, h, s: (b, ns - 1 - s, 0)),
                  pl.BlockSpec((1, LANES, tb), lambda b, h, s: (b, 0, ns - 1 - s)),
                  state_spec,
                  pl.BlockSpec(masks.shape, lambda b, h, s: (0, 0, 0)),
                  pl.BlockSpec(merge_masks.shape, lambda b, h, s: (0, 0, 0))],
        out_specs=[fwd(DN_DV, 0), bwd(DN_DV, 0), state_spec],
        out_shape=[o_shape, o_shape, jax.ShapeDtypeStruct(s0.shape, F32)],
        scratch_shapes=[pltpu.VMEM((hpg, 2, DN_DK, DN_DV), F32)],
        compiler_params=_params(("parallel", "parallel", "arbitrary")),
        name="delta_scan",
    )(qkv, qkv, qkv, gates, gates_t, qkv, qkv, qkv, gates, gates_t, s0, masks, merge_masks)


def _gla_prep_group(qs, ks, lgs, revs, m_ref):
    in_hi = lax.broadcasted_iota(jnp.int32, (CHUNK, GLA_DK), 0) >= SUB
    bc = [_masked_sums(m_ref[2 * rev], lg) for lg, rev in zip(lgs, revs)]

    def per_block(x, rows):
        return jnp.concatenate([jnp.broadcast_to(x[r:r + 1], (SUB, GLA_DK)) for r in rows], axis=0)

    ref_rows = [(SUB // 2 - 1, SUB + SUB // 2 - 1) if rev else (SUB // 2, SUB + SUB // 2) for rev in revs]
    last_rows = [(0, SUB) if rev else (SUB - 1, CHUNK - 1) for rev in revs]
    refb = [per_block(x, rows) for x, rows in zip(bc, ref_rows)]
    lastb = [per_block(x, rows) for x, rows in zip(bc, last_rows)]
    qf = [q.astype(F32) * (GLA_DK ** -0.5) for q in qs]
    kf = [k.astype(F32) for k in ks]
    qe = [q * jnp.exp(x - r) for q, x, r in zip(qf, bc, refb)]
    ke = [k * jnp.exp(r - x) for k, x, r in zip(kf, bc, refb)]
    qd = [q * jnp.exp(x) for q, x in zip(qf, bc)]
    kd = [k * jnp.exp(l - x) for k, x, l in zip(kf, bc, lastb)]
    a_in = [_mm_nt(x, y) for x, y in zip(qe, ke)]
    a_x = [_mm_nt(x, y) for x, y in zip(qd, kd)]
    a = [x * m_ref[2 * rev] + y * m_ref[2 * rev + 1] for x, y, rev in zip(a_in, a_x, revs)]
    last1 = [x[rows[1]:rows[1] + 1] if rev else x[rows[0]:rows[0] + 1] for x, rows, rev in zip(bc, last_rows, revs)]
    last2 = [x[rows[0]:rows[0] + 1] if rev else x[rows[1]:rows[1] + 1] for x, rows, rev in zip(bc, last_rows, revs)]
    second = [jnp.logical_not(in_hi) if rev else in_hi for rev in revs]
    qd2 = [x * jnp.where(sec, jnp.exp(l1), 1.0) for x, sec, l1 in zip(qd, second, last1)]
    kd2 = [x * jnp.where(sec, 1.0, jnp.exp(l2)) for x, sec, l2 in zip(kd, second, last2)]
    cdcol = [jnp.broadcast_to(jnp.exp(l1 + l2), (GLA_DK, LANES)).T for l1, l2 in zip(last1, last2)]
    lhs = [jnp.concatenate([x.astype(BF16), y.astype(BF16)], axis=1) for x, y in zip(qd2, a)]
    kdt = [x.T.astype(BF16) for x in kd2]
    decay = [jnp.concatenate([x] * (GLA_DV // LANES), axis=1) for x in cdcol]
    return lhs, kdt, decay


def _gla_scan_kernel(qf_ref, kf_ref, vf_ref, gf_ref, qb_ref, kb_ref, vb_ref, gb_ref, s0_ref, m_ref,
                     of_ref, ob_ref, sfin_ref, s_scr, *, nc, par):
    step = pl.program_id(2)

    @pl.when(step == 0)
    def _():
        s_scr[...] = s0_ref[0, 0]

    def body(it, carry):
        rows = ([pl.ds(pl.multiple_of((it * par + p) * CHUNK, CHUNK), CHUNK) for p in range(par)]
                + [pl.ds(pl.multiple_of((nc - 1 - it * par - p) * CHUNK, CHUNK), CHUNK) for p in range(par)])
        refs = [(qf_ref, kf_ref, vf_ref, gf_ref, of_ref)] * par + [(qb_ref, kb_ref, vb_ref, gb_ref, ob_ref)] * par
        revs = [0] * par + [1] * par
        lhs, kdt, decay = _gla_prep_group([r[0][0, rw, :] for r, rw in zip(refs, rows)],
                                          [r[1][0, rw, :] for r, rw in zip(refs, rows)],
                                          [r[3][0, rw, :] for r, rw in zip(refs, rows)], revs, m_ref)
        vs = [r[2][0, rw, :] for r, rw in zip(refs, rows)]
        upd = [jnp.dot(x, v, preferred_element_type=F32) for x, v in zip(kdt, vs)]
        state = [s_scr[0], s_scr[1]]
        for p in range(par):
            idx = [p, par + p]
            outs = [jnp.dot(lhs[n], jnp.concatenate([state[d].astype(BF16), vs[n]], axis=0),
                            preferred_element_type=F32) for d, n in enumerate(idx)]
            state = [decay[n] * state[d] + upd[n] for d, n in enumerate(idx)]
            for d, n in enumerate(idx):
                refs[n][4][0, rows[n], :] = outs[d].astype(refs[n][4].dtype)
        s_scr[0] = state[0]
        s_scr[1] = state[1]
        return carry

    lax.fori_loop(0, nc // par, body, 0)

    @pl.when(step == pl.num_programs(2) - 1)
    def _():
        sfin_ref[0, 0] = s_scr[...]


def _gla_scan(p, lg, s0, masks):
    bsz, t, _ = p.shape
    tb = min(t, GLA_SCAN_BLOCK)
    ns = t // tb
    q_off = GLA_V // GLA_DK
    k_off = q_off + GLA_HEADS
    v_off = (GLA_V + 2 * GLA_K) // GLA_DV

    def spec(width, off, rev):
        if rev:
            return pl.BlockSpec((1, tb, width), lambda b, h, s: (b, ns - 1 - s, off + h))
        return pl.BlockSpec((1, tb, width), lambda b, h, s: (b, s, off + h))

    state_spec = pl.BlockSpec((1, 1, 2, GLA_DK, GLA_DV), lambda b, h, s: (b, h, 0, 0, 0))
    o_shape = jax.ShapeDtypeStruct((bsz, t, GLA_V), BF16)
    return pl.pallas_call(
        functools.partial(_gla_scan_kernel, nc=tb // CHUNK, par=min(GLA_SCAN_PAR, tb // CHUNK)),
        grid=(bsz, GLA_HEADS, ns),
        in_specs=[spec(GLA_DK, q_off, 0), spec(GLA_DK, k_off, 0), spec(GLA_DV, v_off, 0), spec(GLA_DK, 0, 0),
                  spec(GLA_DK, q_off, 1), spec(GLA_DK, k_off, 1), spec(GLA_DV, v_off, 1),
                  spec(GLA_DK, GLA_HEADS, 1),
                  state_spec,
                  pl.BlockSpec(masks.shape, lambda b, h, s: (0, 0, 0))],
        out_specs=[spec(GLA_DV, 0, 0), spec(GLA_DV, 0, 1), state_spec],
        out_shape=[o_shape, o_shape, jax.ShapeDtypeStruct(s0.shape, F32)],
        scratch_shapes=[pltpu.VMEM((2, GLA_DK, GLA_DV), F32)],
        compiler_params=_params(("parallel", "parallel", "arbitrary")),
        name="gla_scan",
    )(p, p, p, lg, p, p, p, lg, s0, masks)


def _gated_out_kernel(of_ref, ob_ref, z_ref, x_ref, gt_ref, ng_ref, w_ref, fg_ref, o_ref, *, dv, final):
    o = of_ref[0].astype(F32) + ob_ref[0].astype(F32)
    parts = []
    for s in range(o.shape[1] // dv):
        seg = o[:, s * dv:(s + 1) * dv]
        ms = jnp.mean(seg * seg, axis=-1, keepdims=True)
        parts.append(seg * lax.rsqrt(ms + EPS))
    y = jnp.concatenate(parts, axis=1) * ng_ref[...] * _silu(z_ref[0].astype(F32))
    proj = jnp.dot(y.astype(BF16), w_ref[...], preferred_element_type=F32)
    xn = x_ref[0] + gt_ref[0] * proj
    if final:
        ms = jnp.mean(xn * xn, axis=-1, keepdims=True)
        xn = xn * lax.rsqrt(ms + EPS) * fg_ref[...]
    o_ref[0] = xn


def _gated_out(o_f, o_b, p, z_block, x, gt, norm_g_row, w_out, final_g_row, dv, final):
    bsz, t, d = x.shape
    dvt = o_f.shape[2]
    tm = min(t, 512)
    const = lambda b, i: (0, 0)
    return pl.pallas_call(
        functools.partial(_gated_out_kernel, dv=dv, final=final),
        grid=(bsz, t // tm),
        in_specs=[pl.BlockSpec((1, tm, dvt), lambda b, i: (b, i, 0)),
                  pl.BlockSpec((1, tm, dvt), lambda b, i: (b, i, 0)),
                  pl.BlockSpec((1, tm, dvt), lambda b, i: (b, i, z_block)),
                  pl.BlockSpec((1, tm, d), lambda b, i: (b, i, 0)),
                  pl.BlockSpec((1, 1, d), lambda b, i: (b, 0, 0)),
                  pl.BlockSpec((1, dvt), const),
                  pl.BlockSpec((dvt, d), const),
                  pl.BlockSpec((1, d), const)],
        out_specs=pl.BlockSpec((1, tm, d), lambda b, i: (b, i, 0)),
        out_shape=jax.ShapeDtypeStruct((bsz, t, d), F32),
        compiler_params=_params(("parallel", "parallel")),
        name="gated_out_final" if final else "gated_out",
    )(o_f, o_b, p, x, gt, norm_g_row, w_out, final_g_row)


def _pad_cols(w, width):
    return jnp.pad(w, ((0, 0), (0, width - w.shape[1])))


def kernel(x, c, ctx, c_ctx, mod_w, mod_b, norm_g, dn_w_in, dn_conv_w, dn_a_log, dn_dt_bias, dn_norm_g,
           dn_w_out, gla_w_in, gla_w_g2, gla_b_g, gla_norm_g, gla_w_out, final_g):
    bsz, _, d = x.shape
    assert bsz + 1 <= 8
    cond = jnp.concatenate([c, c_ctx[None], jnp.zeros((8 - bsz - 1, d), F32)], axis=0)
    mod = _modulation(cond, mod_w, mod_b)

    def mod_rows(layer):
        lat = [mod[layer, :bsz, None, n * d:(n + 1) * d] for n in range(3)]
        cx = [jnp.broadcast_to(mod[layer, bsz, n * d:(n + 1) * d], (bsz, 1, d)) for n in range(3)]
        return lat, cx

    final_row = final_g.reshape(1, d)

    (sh_l, sc_l, gt_l), (sh_c, sc_c, gt_c) = mod_rows(0)
    w0 = dn_w_in[0].astype(BF16)
    w0_main, w0_small = w0[:, :DN_MAIN], _pad_cols(w0[:, DN_MAIN:], LANES)
    gate_params = jnp.zeros((8, LANES), F32)
    gate_params = gate_params.at[0, :2 * DN_HEADS].set(dn_a_log[0].reshape(-1))
    gate_params = gate_params.at[1, :2 * DN_HEADS].set(dn_dt_bias[0].reshape(-1))
    g0 = norm_g[0].reshape(1, d)
    conv_w9 = dn_conv_w[0].reshape(9, DN_QKV)
    dmasks = (_delta_mask_array(), _delta_merge_mask_array())
    dn_g_row = jnp.tile(dn_norm_g[0], DN_HEADS).reshape(1, DN_V)
    dn_wo = dn_w_out[0].astype(BF16)

    p_c, gates_c, gates_tc = _inproj(_dn_inproj_kernel, "dn_inproj", ctx, sh_c, sc_c, g0, w0_main, w0_small,
                                     [gate_params], LANES, aux_transposed=True)
    qkv_c = _conv_seq(p_c, conv_w9)
    qkv_l, z_l, gates_l, gates_tl = _dn_inproj_conv(x, sh_l, sc_l, g0, w0_main, w0_small, gate_params, conv_w9)
    s_zero = jnp.zeros((bsz, DN_HEADS, 2, DN_DK, DN_DV), F32)
    oc_f, oc_b, s_ctx = _delta_scan(qkv_c, gates_c, gates_tc, s_zero, *dmasks)
    ol_f, ol_b, _ = _delta_scan(qkv_l, gates_l, gates_tl, s_ctx, *dmasks)
    ctx1 = _gated_out(oc_f, oc_b, p_c, DN_QKV // DN_V, ctx, gt_c, dn_g_row, dn_wo, final_row, DN_DV, False)
    x1 = _gated_out(ol_f, ol_b, z_l, 0, x, gt_l, dn_g_row, dn_wo, final_row, DN_DV, False)

    (sh_l, sc_l, gt_l), (sh_c, sc_c, _) = mod_rows(1)
    w1 = gla_w_in[0].astype(BF16)
    w1_main = jnp.concatenate([w1[:, 2 * GLA_K + GLA_V:GLA_MAIN], w1[:, :2 * GLA_K + GLA_V]], axis=1)
    w1_small = _pad_cols(w1[:, GLA_MAIN:], LANES)
    wg = jnp.zeros((LANES, 2 * GLA_K), F32)
    wg = wg.at[:GLA_RANK, :GLA_K].set(gla_w_g2[0, 0]).at[GLA_RANK:2 * GLA_RANK, GLA_K:].set(gla_w_g2[0, 1])
    wg = wg.astype(BF16)
    bg = gla_b_g[0].reshape(1, 2 * GLA_K)
    g1 = norm_g[1].reshape(1, d)
    gmasks = _gla_mask_array()
    gla_g_row = jnp.tile(gla_norm_g[0], GLA_HEADS).reshape(1, GLA_V)
    gla_wo = gla_w_out[0].astype(BF16)

    def gla_features(u, sh, sc):
        return _gla_inproj(u, sh, sc, g1, w1_main, w1_small, wg, bg)

    pg_c, lg_c = gla_features(ctx1, sh_c, sc_c)
    pg_l, lg_l = gla_features(x1, sh_l, sc_l)
    s_zero = jnp.zeros((bsz, GLA_HEADS, 2, GLA_DK, GLA_DV), F32)
    _, _, s_ctx = _gla_scan(pg_c, lg_c, s_zero, gmasks)
    og_f, og_b, _ = _gla_scan(pg_l, lg_l, s_ctx, gmasks)
    return _gated_out(og_f, og_b, pg_l, 0, x1, gt_l, gla_g_row, gla_wo, final_row, GLA_DV, True)
```
